```python
import math
import jax, jax.numpy as jnp
from jax import lax
import numpy as np

D_MODEL = 2048
BATCH = 4
SEQ = 4096
DEPTH = 1

N_MEM = 256
HEAD_DIM = 128
GDN_HEADS = 8
SB_HEADS = 4
MEM_HEADS = 4
GDN_WIDTH = GDN_HEADS * HEAD_DIM
SB_WIDTH = SB_HEADS * HEAD_DIM
MEM_WIDTH = MEM_HEADS * HEAD_DIM
MIX_WIDTH = GDN_WIDTH + SB_WIDTH + MEM_WIDTH
IN_COLS = 4 * GDN_WIDTH + 2 * GDN_HEADS + 3 * SB_WIDTH + MEM_WIDTH
CONV_WIDTH = 4
GDN_CHUNK = 64
SB_BLOCK = 128
PEER_HEADS = 8
PEER_N_KEYS = 128
PEER_N_EXPERTS = PEER_N_KEYS * PEER_N_KEYS
PEER_TOPK = 16
PEER_D_HALF = 128
PEER_D_QUERY = 2 * PEER_D_HALF
PEER_TOKEN_BLOCK = 128
DEEPNORM_ALPHA = (2 * DEPTH) ** 0.25
DEEPNORM_BETA = (8 * DEPTH) ** -0.25
LN_EPS = 1e-5
RMS_EPS = 1e-6

kernel_name = 'hybrid_gdn_stickbreak_mem_peer_deepnorm'


def layer_norm(x, w, b):
    xf = x.astype(jnp.float32)
    mu = jnp.mean(xf, axis=-1, keepdims=True)
    var = jnp.mean(jnp.square(xf - mu), axis=-1, keepdims=True)
    return ((xf - mu) * lax.rsqrt(var + LN_EPS) * w.astype(jnp.float32) + b.astype(jnp.float32)).astype(x.dtype)


def rms_norm(x, w):
    xf = x.astype(jnp.float32)
    return xf * lax.rsqrt(jnp.mean(jnp.square(xf), axis=-1, keepdims=True) + RMS_EPS) * w.astype(jnp.float32)


def l2_normalize(x):
    return x * lax.rsqrt(jnp.sum(jnp.square(x), axis=-1, keepdims=True) + RMS_EPS)


def causal_conv(x, w):
    T = x.shape[1]
    xp = jnp.pad(x, ((0, 0), (CONV_WIDTH - 1, 0), (0, 0)))
    return sum(xp[:, i:i + T] * w[i] for i in range(CONV_WIDTH))


def to_heads(t, n_heads):
    B, T, _ = t.shape
    return t.reshape(B, T, n_heads, HEAD_DIM).transpose(0, 2, 1, 3)


def gated_delta_rule_chunked(q, k, v, g, beta):
    f32 = jnp.float32
    q, k, v, g, beta = (t.astype(f32) for t in (q, k, v, g, beta))
    B, H, T, Dk = q.shape
    Dv = v.shape[-1]
    C = GDN_CHUNK
    N = T // C
    q = q.reshape(B, H, N, C, Dk)
    k = k.reshape(B, H, N, C, Dk)
    v = v.reshape(B, H, N, C, Dv)
    beta = beta.reshape(B, H, N, C)
    g = jnp.cumsum(g.reshape(B, H, N, C), axis=-1)
    k_beta = k * beta[..., None]
    v_beta = v * beta[..., None]
    tril = jnp.tril(jnp.ones((C, C), dtype=bool))
    strict = jnp.tril(jnp.ones((C, C), dtype=bool), -1)
    decay = jnp.exp(jnp.where(tril, g[..., :, None] - g[..., None, :], -jnp.inf))
    lower = jnp.where(strict, jnp.einsum('bhnid,bhnjd->bhnij', k_beta, k) * decay, 0.0)
    a_mat = lower + jnp.eye(C, dtype=f32)
    u = lax.linalg.triangular_solve(a_mat, v_beta, left_side=True, lower=True, unit_diagonal=True)
    w = lax.linalg.triangular_solve(a_mat, k_beta * jnp.exp(g)[..., None], left_side=True, lower=True, unit_diagonal=True)
    qk = jnp.einsum('bhnid,bhnjd->bhnij', q, k) * decay

    def step(S, inp):
        q_c, k_c, u_c, w_c, g_c, qk_c = inp
        v_new = u_c - jnp.einsum('bhcd,bhde->bhce', w_c, S)
        o = jnp.einsum('bhcd,bhde->bhce', q_c * jnp.exp(g_c)[..., None], S) + jnp.einsum('bhij,bhje->bhie', qk_c, v_new)
        g_last = g_c[..., -1]
        S = S * jnp.exp(g_last)[..., None, None] + jnp.einsum('bhcd,bhce->bhde', k_c * jnp.exp(g_last[..., None] - g_c)[..., None], v_new)
        return S, o

    xs = tuple(jnp.moveaxis(t, 2, 0) for t in (q, k, u, w, g, qk))
    S0 = jnp.zeros((B, H, Dk, Dv), f32)
    _, o = lax.scan(step, S0, xs)
    return jnp.moveaxis(o, 0, 2).reshape(B, H, T, Dv)


def stick_breaking_attention(q, k, v):
    T = q.shape[2]
    scale = HEAD_DIM ** -0.5
    outs = []
    for blk in range(T // SB_BLOCK):
        t0 = blk * SB_BLOCK
        t1 = t0 + SB_BLOCK
        qb = q[:, :, t0:t1]
        kb = k[:, :, :t1]
        vb = v[:, :, :t1]
        z = jnp.einsum('bhqd,bhkd->bhqk', qb, kb).astype(jnp.float32) * scale
        causal = jnp.arange(t1)[None, :] < (t0 + jnp.arange(SB_BLOCK))[:, None]
        log_not = jnp.where(causal, -jax.nn.softplus(z), 0.0)
        after = lax.cumsum(log_not, axis=3, reverse=True) - log_not
        a = jnp.where(causal, jnp.exp(jax.nn.log_sigmoid(z) + after), 0.0)
        outs.append(jnp.einsum('bhqk,bhkd->bhqd', a.astype(vb.dtype), vb))
    return jnp.concatenate(outs, axis=2)


def mixing_block(h, mem, w_in, conv_w, a_log, dt_bias, gdn_norm_w, sb_norm_w, mem_norm_w,
                 ln_mem_w, ln_mem_b, w_mem_kv, w_out):
    B, T, _ = h.shape
    f32 = jnp.float32
    proj = h @ w_in
    o1 = 3 * GDN_WIDTH
    o2 = o1 + GDN_WIDTH
    o3 = o2 + GDN_HEADS
    o4 = o3 + GDN_HEADS
    o5 = o4 + 3 * SB_WIDTH
    gdn_qkv, gdn_z, gdn_a, gdn_b, sb_qkv, mem_q = jnp.split(proj, [o1, o2, o3, o4, o5], axis=-1)

    qkv = jax.nn.silu(causal_conv(gdn_qkv, conv_w))
    q, k, v = (to_heads(t, GDN_HEADS) for t in jnp.split(qkv, 3, axis=-1))
    q = l2_normalize(q.astype(f32)) * HEAD_DIM ** -0.5
    k = l2_normalize(k.astype(f32))
    g = -jnp.exp(a_log.astype(f32)) * jax.nn.softplus(gdn_a.astype(f32) + dt_bias.astype(f32))
    beta = jax.nn.sigmoid(gdn_b.astype(f32))
    o_gdn = gated_delta_rule_chunked(q, k, v, g.transpose(0, 2, 1), beta.transpose(0, 2, 1))
    o_gdn = o_gdn.transpose(0, 2, 1, 3)
    z = gdn_z.reshape(B, T, GDN_HEADS, HEAD_DIM).astype(f32)
    o_gdn = (rms_norm(o_gdn, gdn_norm_w) * jax.nn.silu(z)).reshape(B, T, GDN_WIDTH)

    q_sb, k_sb, v_sb = (to_heads(t, SB_HEADS) for t in jnp.split(sb_qkv, 3, axis=-1))
    o_sb = stick_breaking_attention(q_sb, k_sb, v_sb).transpose(0, 2, 1, 3)
    o_sb = rms_norm(o_sb, sb_norm_w.reshape(SB_HEADS, HEAD_DIM)).reshape(B, T, SB_WIDTH)

    m = layer_norm(mem, ln_mem_w, ln_mem_b)
    mk, mv = jnp.split(m @ w_mem_kv, 2, axis=-1)
    mk = mk.reshape(B, N_MEM, MEM_HEADS, HEAD_DIM)
    mv = mv.reshape(B, N_MEM, MEM_HEADS, HEAD_DIM)
    qm = mem_q.reshape(B, T, MEM_HEADS, HEAD_DIM)
    s = jnp.einsum('bthd,bmhd->bhtm', qm, mk).astype(f32) * HEAD_DIM ** -0.5
    p = jax.nn.softmax(s, axis=-1)
    o_mem = jnp.einsum('bhtm,bmhd->bthd', p.astype(mv.dtype), mv)
    o_mem = rms_norm(o_mem, mem_norm_w.reshape(MEM_HEADS, HEAD_DIM)).reshape(B, T, MEM_WIDTH)

    o = jnp.concatenate([o_gdn, o_sb, o_mem], axis=-1).astype(h.dtype)
    return o @ w_out


def peer_ffn(h, wq, keys1, keys2, u, v):
    B, T, D = h.shape
    TB = PEER_TOKEN_BLOCK
    nb = (B * T) // TB
    xt = h.reshape(B * T, D)
    q = (xt @ wq).reshape(B * T, PEER_HEADS, 2, PEER_D_HALF)
    s1 = jnp.einsum('thd,kd->thk', q[:, :, 0], keys1).astype(jnp.float32)
    s2 = jnp.einsum('thd,kd->thk', q[:, :, 1], keys2).astype(jnp.float32)
    v1, i1 = lax.top_k(s1, PEER_TOPK)
    v2, i2 = lax.top_k(s2, PEER_TOPK)
    cand_s = (v1[..., :, None] + v2[..., None, :]).reshape(B * T, PEER_HEADS, PEER_TOPK * PEER_TOPK)
    cand_i = (i1[..., :, None] * PEER_N_KEYS + i2[..., None, :]).reshape(B * T, PEER_HEADS, PEER_TOPK * PEER_TOPK)
    top_s, pos = lax.top_k(cand_s, PEER_TOPK)
    idx = jnp.take_along_axis(cand_i, pos, axis=-1)
    gate = jax.nn.softmax(top_s, axis=-1)

    def expert_block(args):
        xb, ib, gb = args
        ub = jnp.take(u, ib, axis=0)
        act = jax.nn.gelu(jnp.einsum('td,thkd->thk', xb, ub).astype(jnp.float32), approximate=False)
        wgt = (gb * act).astype(v.dtype)
        return jnp.einsum('thk,thkd->td', wgt, jnp.take(v, ib, axis=0))

    out = lax.map(expert_block, (xt.reshape(nb, TB, D),
                                 idx.reshape(nb, TB, PEER_HEADS, PEER_TOPK),
                                 gate.reshape(nb, TB, PEER_HEADS, PEER_TOPK)))
    return out.reshape(B, T, D).astype(h.dtype)


def setup_inputs(seed: int = 0) -> dict:
    key = jax.random.key(seed)
    ks = jax.random.split(key, 32)
    f32 = jnp.float32
    L = DEPTH

    def nrm(k, shape, scale):
        return jax.random.normal(k, shape, f32) * scale

    def gain(k, shape):
        return 1.0 + 0.02 * jax.random.normal(k, shape, f32)

    def bias(k, shape):
        return 0.02 * jax.random.normal(k, shape, f32)

    dt = jnp.exp(jax.random.uniform(ks[6], (L, GDN_HEADS), f32, math.log(1e-3), math.log(1e-1)))
    return {
        'x': nrm(ks[0], (BATCH, SEQ, D_MODEL), 1.0),
        'mem': nrm(ks[1], (BATCH, N_MEM, D_MODEL), 1.0),
        'ln_emb_w': gain(ks[2], (D_MODEL,)),
        'ln_emb_b': bias(ks[3], (D_MODEL,)),
        'w_in': nrm(ks[4], (L, D_MODEL, IN_COLS), D_MODEL ** -0.5),
        'conv_w': nrm(ks[5], (L, CONV_WIDTH, 3 * GDN_WIDTH), CONV_WIDTH ** -0.5),
        'a_log': jnp.log(jax.random.uniform(ks[7], (L, GDN_HEADS), f32, 1.0, 16.0)),
        'dt_bias': dt + jnp.log(-jnp.expm1(-dt)),
        'gdn_norm_w': gain(ks[8], (L, HEAD_DIM)),
        'sb_norm_w': gain(ks[9], (L, SB_WIDTH)),
        'mem_norm_w': gain(ks[10], (L, MEM_WIDTH)),
        'ln_mem_w': gain(ks[11], (L, D_MODEL)),
        'ln_mem_b': bias(ks[12], (L, D_MODEL)),
        'w_mem_kv': nrm(ks[13], (L, D_MODEL, 2 * MEM_WIDTH), D_MODEL ** -0.5),
        'w_out': nrm(ks[14], (L, MIX_WIDTH, D_MODEL), MIX_WIDTH ** -0.5 * DEEPNORM_BETA),
        'ln1_w': gain(ks[15], (L, D_MODEL)),
        'ln1_b': bias(ks[16], (L, D_MODEL)),
        'peer_wq': nrm(ks[17], (L, D_MODEL, PEER_HEADS * PEER_D_QUERY), D_MODEL ** -0.5),
        'peer_keys1': nrm(ks[18], (L, PEER_N_KEYS, PEER_D_HALF), PEER_D_HALF ** -0.5),
        'peer_keys2': nrm(ks[19], (L, PEER_N_KEYS, PEER_D_HALF), PEER_D_HALF ** -0.5),
        'peer_u': nrm(ks[20], (L, PEER_N_EXPERTS, D_MODEL), D_MODEL ** -0.5),
        'peer_v': nrm(ks[21], (L, PEER_N_EXPERTS, D_MODEL), DEEPNORM_BETA),
        'ln2_w': gain(ks[22], (L, D_MODEL)),
        'ln2_b': bias(ks[23], (L, D_MODEL)),
    }


def reference(x, mem, ln_emb_w, ln_emb_b, w_in, conv_w, a_log, dt_bias, gdn_norm_w, sb_norm_w,
              mem_norm_w, ln_mem_w, ln_mem_b, w_mem_kv, w_out, ln1_w, ln1_b, peer_wq, peer_keys1,
              peer_keys2, peer_u, peer_v, ln2_w, ln2_b):
    h = layer_norm(x, ln_emb_w, ln_emb_b)
    for l in range(DEPTH):
        mix = mixing_block(h, mem, w_in[l], conv_w[l], a_log[l], dt_bias[l], gdn_norm_w[l], sb_norm_w[l],
                           mem_norm_w[l], ln_mem_w[l], ln_mem_b[l], w_mem_kv[l], w_out[l])
        h = layer_norm(DEEPNORM_ALPHA * h + mix, ln1_w[l], ln1_b[l])
        ffn = peer_ffn(h, peer_wq[l], peer_keys1[l], peer_keys2[l], peer_u[l], peer_v[l])
        h = layer_norm(DEEPNORM_ALPHA * h + ffn, ln2_w[l], ln2_b[l])
    return h
```

```python
import jax, jax.numpy as jnp
from jax import lax
from jax.experimental import pallas as pl

D_MODEL = 2048
BATCH = 4
SEQ = 4096
DEPTH = 1

N_MEM = 256
HEAD_DIM = 128
GDN_HEADS = 8
SB_HEADS = 4
MEM_HEADS = 4
GDN_WIDTH = GDN_HEADS * HEAD_DIM
SB_WIDTH = SB_HEADS * HEAD_DIM
MEM_WIDTH = MEM_HEADS * HEAD_DIM
MIX_WIDTH = GDN_WIDTH + SB_WIDTH + MEM_WIDTH
IN_COLS = 4 * GDN_WIDTH + 2 * GDN_HEADS + 3 * SB_WIDTH + MEM_WIDTH
CONV_WIDTH = 4
GDN_CHUNK = 64
SB_BLOCK = 128
PEER_HEADS = 8
PEER_N_KEYS = 128
PEER_N_EXPERTS = PEER_N_KEYS * PEER_N_KEYS
PEER_TOPK = 16
PEER_D_HALF = 128
PEER_D_QUERY = 2 * PEER_D_HALF
PEER_TOKEN_BLOCK = 128
DEEPNORM_ALPHA = (2 * DEPTH) ** 0.25
DEEPNORM_BETA = (8 * DEPTH) ** -0.25
LN_EPS = 1e-5
RMS_EPS = 1e-6


def layer_norm(x, w, b):
    xf = x.astype(jnp.float32)
    mu = jnp.mean(xf, axis=-1, keepdims=True)
    var = jnp.mean(jnp.square(xf - mu), axis=-1, keepdims=True)
    return ((xf - mu) * lax.rsqrt(var + LN_EPS) * w.astype(jnp.float32) + b.astype(jnp.float32)).astype(x.dtype)


def rms_norm(x, w):
    xf = x.astype(jnp.float32)
    return xf * lax.rsqrt(jnp.mean(jnp.square(xf), axis=-1, keepdims=True) + RMS_EPS) * w.astype(jnp.float32)


def l2_normalize(x):
    return x * lax.rsqrt(jnp.sum(jnp.square(x), axis=-1, keepdims=True) + RMS_EPS)


def causal_conv(x, w):
    T = x.shape[1]
    xp = jnp.pad(x, ((0, 0), (CONV_WIDTH - 1, 0), (0, 0)))
    return sum(xp[:, i:i + T] * w[i] for i in range(CONV_WIDTH))


def to_heads(t, n_heads):
    B, T, _ = t.shape
    return t.reshape(B, T, n_heads, HEAD_DIM).transpose(0, 2, 1, 3)


def gated_delta_rule_chunked(q, k, v, g, beta):
    f32 = jnp.float32
    q, k, v, g, beta = (t.astype(f32) for t in (q, k, v, g, beta))
    B, H, T, Dk = q.shape
    Dv = v.shape[-1]
    C = GDN_CHUNK
    N = T // C
    q = q.reshape(B, H, N, C, Dk)
    k = k.reshape(B, H, N, C, Dk)
    v = v.reshape(B, H, N, C, Dv)
    beta = beta.reshape(B, H, N, C)
    g = jnp.cumsum(g.reshape(B, H, N, C), axis=-1)
    k_beta = k * beta[..., None]
    v_beta = v * beta[..., None]
    tril = jnp.tril(jnp.ones((C, C), dtype=bool))
    strict = jnp.tril(jnp.ones((C, C), dtype=bool), -1)
    decay = jnp.exp(jnp.where(tril, g[..., :, None] - g[..., None, :], -jnp.inf))
    lower = jnp.where(strict, jnp.einsum('bhnid,bhnjd->bhnij', k_beta, k) * decay, 0.0)
    a_mat = lower + jnp.eye(C, dtype=f32)
    u = lax.linalg.triangular_solve(a_mat, v_beta, left_side=True, lower=True, unit_diagonal=True)
    w = lax.linalg.triangular_solve(a_mat, k_beta * jnp.exp(g)[..., None], left_side=True, lower=True, unit_diagonal=True)
    qk = jnp.einsum('bhnid,bhnjd->bhnij', q, k) * decay

    def step(S, inp):
        q_c, k_c, u_c, w_c, g_c, qk_c = inp
        v_new = u_c - jnp.einsum('bhcd,bhde->bhce', w_c, S)
        o = jnp.einsum('bhcd,bhde->bhce', q_c * jnp.exp(g_c)[..., None], S) + jnp.einsum('bhij,bhje->bhie', qk_c, v_new)
        g_last = g_c[..., -1]
        S = S * jnp.exp(g_last)[..., None, None] + jnp.einsum('bhcd,bhce->bhde', k_c * jnp.exp(g_last[..., None] - g_c)[..., None], v_new)
        return S, o

    xs = tuple(jnp.moveaxis(t, 2, 0) for t in (q, k, u, w, g, qk))
    S0 = jnp.zeros((B, H, Dk, Dv), f32)
    _, o = lax.scan(step, S0, xs)
    return jnp.moveaxis(o, 0, 2).reshape(B, H, T, Dv)


def stick_breaking_attention(q, k, v):
    T = q.shape[2]
    scale = HEAD_DIM ** -0.5
    outs = []
    for blk in range(T // SB_BLOCK):
        t0 = blk * SB_BLOCK
        t1 = t0 + SB_BLOCK
        qb = q[:, :, t0:t1]
        kb = k[:, :, :t1]
        vb = v[:, :, :t1]
        z = jnp.einsum('bhqd,bhkd->bhqk', qb, kb).astype(jnp.float32) * scale
        causal = jnp.arange(t1)[None, :] < (t0 + jnp.arange(SB_BLOCK))[:, None]
        log_not = jnp.where(causal, -jax.nn.softplus(z), 0.0)
        after = lax.cumsum(log_not, axis=3, reverse=True) - log_not
        a = jnp.where(causal, jnp.exp(jax.nn.log_sigmoid(z) + after), 0.0)
        outs.append(jnp.einsum('bhqk,bhkd->bhqd', a.astype(vb.dtype), vb))
    return jnp.concatenate(outs, axis=2)


def mixing_block(h, mem, w_in, conv_w, a_log, dt_bias, gdn_norm_w, sb_norm_w, mem_norm_w,
                 ln_mem_w, ln_mem_b, w_mem_kv, w_out):
    B, T, _ = h.shape
    f32 = jnp.float32
    proj = h @ w_in
    o1 = 3 * GDN_WIDTH
    o2 = o1 + GDN_WIDTH
    o3 = o2 + GDN_HEADS
    o4 = o3 + GDN_HEADS
    o5 = o4 + 3 * SB_WIDTH
    gdn_qkv, gdn_z, gdn_a, gdn_b, sb_qkv, mem_q = jnp.split(proj, [o1, o2, o3, o4, o5], axis=-1)

    qkv = jax.nn.silu(causal_conv(gdn_qkv, conv_w))
    q, k, v = (to_heads(t, GDN_HEADS) for t in jnp.split(qkv, 3, axis=-1))
    q = l2_normalize(q.astype(f32)) * HEAD_DIM ** -0.5
    k = l2_normalize(k.astype(f32))
    g = -jnp.exp(a_log.astype(f32)) * jax.nn.softplus(gdn_a.astype(f32) + dt_bias.astype(f32))
    beta = jax.nn.sigmoid(gdn_b.astype(f32))
    o_gdn = gated_delta_rule_chunked(q, k, v, g.transpose(0, 2, 1), beta.transpose(0, 2, 1))
    o_gdn = o_gdn.transpose(0, 2, 1, 3)
    z = gdn_z.reshape(B, T, GDN_HEADS, HEAD_DIM).astype(f32)
    o_gdn = (rms_norm(o_gdn, gdn_norm_w) * jax.nn.silu(z)).reshape(B, T, GDN_WIDTH)

    q_sb, k_sb, v_sb = (to_heads(t, SB_HEADS) for t in jnp.split(sb_qkv, 3, axis=-1))
    o_sb = stick_breaking_attention(q_sb, k_sb, v_sb).transpose(0, 2, 1, 3)
    o_sb = rms_norm(o_sb, sb_norm_w.reshape(SB_HEADS, HEAD_DIM)).reshape(B, T, SB_WIDTH)

    m = layer_norm(mem, ln_mem_w, ln_mem_b)
    mk, mv = jnp.split(m @ w_mem_kv, 2, axis=-1)
    mk = mk.reshape(B, N_MEM, MEM_HEADS, HEAD_DIM)
    mv = mv.reshape(B, N_MEM, MEM_HEADS, HEAD_DIM)
    qm = mem_q.reshape(B, T, MEM_HEADS, HEAD_DIM)
    s = jnp.einsum('bthd,bmhd->bhtm', qm, mk).astype(f32) * HEAD_DIM ** -0.5
    p = jax.nn.softmax(s, axis=-1)
    o_mem = jnp.einsum('bhtm,bmhd->bthd', p.astype(mv.dtype), mv)
    o_mem = rms_norm(o_mem, mem_norm_w.reshape(MEM_HEADS, HEAD_DIM)).reshape(B, T, MEM_WIDTH)

    o = jnp.concatenate([o_gdn, o_sb, o_mem], axis=-1).astype(h.dtype)
    return o @ w_out


def peer_ffn(h, wq, keys1, keys2, u, v):
    B, T, D = h.shape
    TB = PEER_TOKEN_BLOCK
    nb = (B * T) // TB
    xt = h.reshape(B * T, D)
    q = (xt @ wq).reshape(B * T, PEER_HEADS, 2, PEER_D_HALF)
    s1 = jnp.einsum('thd,kd->thk', q[:, :, 0], keys1).astype(jnp.float32)
    s2 = jnp.einsum('thd,kd->thk', q[:, :, 1], keys2).astype(jnp.float32)
    v1, i1 = lax.top_k(s1, PEER_TOPK)
    v2, i2 = lax.top_k(s2, PEER_TOPK)
    cand_s = (v1[..., :, None] + v2[..., None, :]).reshape(B * T, PEER_HEADS, PEER_TOPK * PEER_TOPK)
    cand_i = (i1[..., :, None] * PEER_N_KEYS + i2[..., None, :]).reshape(B * T, PEER_HEADS, PEER_TOPK * PEER_TOPK)
    top_s, pos = lax.top_k(cand_s, PEER_TOPK)
    idx = jnp.take_along_axis(cand_i, pos, axis=-1)
    gate = jax.nn.softmax(top_s, axis=-1)

    def expert_block(args):
        xb, ib, gb = args
        ub = jnp.take(u, ib, axis=0)
        act = jax.nn.gelu(jnp.einsum('td,thkd->thk', xb, ub).astype(jnp.float32), approximate=False)
        wgt = (gb * act).astype(v.dtype)
        return jnp.einsum('thk,thkd->td', wgt, jnp.take(v, ib, axis=0))

    out = lax.map(expert_block, (xt.reshape(nb, TB, D),
                                 idx.reshape(nb, TB, PEER_HEADS, PEER_TOPK),
                                 gate.reshape(nb, TB, PEER_HEADS, PEER_TOPK)))
    return out.reshape(B, T, D).astype(h.dtype)


def _ln_kernel(x_ref, w_ref, b_ref, o_ref):
    x = x_ref[...]
    mu = jnp.mean(x, axis=-1, keepdims=True)
    xc = x - mu
    var = jnp.mean(xc * xc, axis=-1, keepdims=True)
    o_ref[...] = xc * lax.rsqrt(var + LN_EPS) * w_ref[...] + b_ref[...]


def _pallas_ln(x2d, w, b):
    n, d = x2d.shape
    tb = 512
    return pl.pallas_call(
        _ln_kernel,
        grid=(n // tb,),
        in_specs=[pl.BlockSpec((tb, d), lambda i: (i, 0)),
                  pl.BlockSpec((1, d), lambda i: (0, 0)),
                  pl.BlockSpec((1, d), lambda i: (0, 0))],
        out_specs=pl.BlockSpec((tb, d), lambda i: (i, 0)),
        out_shape=jax.ShapeDtypeStruct((n, d), jnp.float32),
    )(x2d, w.reshape(1, d), b.reshape(1, d))


def kernel(x, mem, ln_emb_w, ln_emb_b, w_in, conv_w, a_log, dt_bias, gdn_norm_w, sb_norm_w,
           mem_norm_w, ln_mem_w, ln_mem_b, w_mem_kv, w_out, ln1_w, ln1_b, peer_wq, peer_keys1,
           peer_keys2, peer_u, peer_v, ln2_w, ln2_b):
    h = layer_norm(x, ln_emb_w, ln_emb_b)
    l = 0
    mix = mixing_block(h, mem, w_in[l], conv_w[l], a_log[l], dt_bias[l], gdn_norm_w[l], sb_norm_w[l],
                       mem_norm_w[l], ln_mem_w[l], ln_mem_b[l], w_mem_kv[l], w_out[l])
    h = layer_norm(DEEPNORM_ALPHA * h + mix, ln1_w[l], ln1_b[l])
    ffn = peer_ffn(h, peer_wq[l], peer_keys1[l], peer_keys2[l], peer_u[l], peer_v[l])
    y = (DEEPNORM_ALPHA * h + ffn).reshape(BATCH * SEQ, D_MODEL)
    return _pallas_ln(y, ln2_w[l], ln2_b[l]).reshape(BATCH, SEQ, D_MODEL)
```

```python
import functools

import jax
import jax.numpy as jnp
from jax import lax
from jax.experimental import pallas as pl
from jax.experimental.pallas import tpu as pltpu

F32 = jnp.float32
BF16 = jnp.bfloat16

HEAD_DIM = 128
GDN_HEADS = 8
SB_HEADS = 4
MEM_HEADS = 4
GDN_WIDTH = GDN_HEADS * HEAD_DIM
SB_WIDTH = SB_HEADS * HEAD_DIM
MEM_WIDTH = MEM_HEADS * HEAD_DIM
CONV_WIDTH = 4
PEER_HEADS = 8
PEER_N_KEYS = 128
PEER_TOPK = 16
PEER_D_HALF = 128
DEPTH = 1
DEEPNORM_ALPHA = (2 * DEPTH) ** 0.25
LN_EPS = 1e-5
RMS_EPS = 1e-6
ATTN_SCALE = HEAD_DIM ** -0.5

COL_GQ = 0
COL_GK = GDN_WIDTH
COL_GV = 2 * GDN_WIDTH
COL_GZ = 3 * GDN_WIDTH
COL_SQ = 4 * GDN_WIDTH
COL_SK = COL_SQ + SB_WIDTH
COL_SV = COL_SK + SB_WIDTH
COL_MQ = COL_SV + SB_WIDTH
COL_AB = COL_MQ + MEM_WIDTH
PROJ_COLS = COL_AB + 128

GDN_CHUNK = 128
HALO = 8
NEG_BIG = -3.0e38
MIB = 1024 * 1024


def _cparams(sem, vmem_mib):
    return pltpu.CompilerParams(dimension_semantics=sem, vmem_limit_bytes=vmem_mib * MIB)


def _dot(a, b):
    return jnp.dot(a, b, preferred_element_type=F32)


def _dot_nt(a, b):
    return lax.dot_general(a, b, (((1,), (1,)), ((), ())), preferred_element_type=F32)


def _split(a):
    hi = a.astype(BF16)
    lo = (a - hi.astype(F32)).astype(BF16)
    return hi, lo


def _dot3(a, b):
    ah, al = _split(a)
    bh, bl = _split(b)
    return _dot(ah, bh) + (_dot(ah, bl) + _dot(al, bh))


def _softplus(x):
    return jnp.maximum(x, 0.0) + jnp.log(1.0 + jnp.exp(-jnp.abs(x)))


def _sigmoid(x):
    return 1.0 / (1.0 + jnp.exp(-x))


def _silu(x):
    return x * _sigmoid(x)


def _layer_norm_rows(x, w, b):
    mu = jnp.mean(x, axis=-1, keepdims=True)
    xc = x - mu
    var = jnp.mean(xc * xc, axis=-1, keepdims=True)
    return xc * lax.rsqrt(var + LN_EPS) * w + b


def _rms_rows(x, w):
    return x * lax.rsqrt(jnp.mean(x * x, axis=-1, keepdims=True) + RMS_EPS) * w


def _ln_kernel(x_ref, w_ref, b_ref, o_ref, ob_ref):
    y = _layer_norm_rows(x_ref[...], w_ref[...], b_ref[...])
    o_ref[...] = y
    ob_ref[...] = y.astype(BF16)


def _ln(x2d, w, b):
    n, d = x2d.shape
    tm = min(512, n)
    return pl.pallas_call(
        _ln_kernel,
        grid=(n // tm,),
        in_specs=[pl.BlockSpec((tm, d), lambda i: (i, 0)),
                  pl.BlockSpec((1, d), lambda i: (0, 0)),
                  pl.BlockSpec((1, d), lambda i: (0, 0))],
        out_specs=[pl.BlockSpec((tm, d), lambda i: (i, 0)),
                   pl.BlockSpec((tm, d), lambda i: (i, 0))],
        out_shape=[jax.ShapeDtypeStruct((n, d), F32), jax.ShapeDtypeStruct((n, d), BF16)],
        compiler_params=_cparams(("parallel",), 32),
        name="layer_norm",
    )(x2d, w.reshape(1, d), b.reshape(1, d))


def _mm_kernel(a_ref, w_ref, o_ref):
    o_ref[...] = _dot(a_ref[...], w_ref[...]).astype(o_ref.dtype)


def _mm(a, w, tn, name):
    m, k = a.shape
    n = w.shape[1]
    tm = min(1024, m)
    return pl.pallas_call(
        _mm_kernel,
        grid=(m // tm, n // tn),
        in_specs=[pl.BlockSpec((tm, k), lambda i, j: (i, 0)),
                  pl.BlockSpec((k, tn), lambda i, j: (0, j))],
        out_specs=pl.BlockSpec((tm, tn), lambda i, j: (i, j)),
        out_shape=jax.ShapeDtypeStruct((m, n), F32),
        compiler_params=_cparams(("parallel", "arbitrary"), 40),
        name=name,
    )(a, w)


def _gdn_kernel(q_ref, k_ref, v_ref, z_ref, ab_ref, cwq_ref, cwk_ref, cwv_ref, alog_ref, dtb_ref, nw_ref,
                o_ref, xbuf, s_ref, *, hps, rows):
    C = GDN_CHUNK
    hg = pl.program_id(1)
    tt = pl.program_id(2)

    @pl.when(tt == 0)
    def _():
        xbuf[:, 0:HALO, :] = jnp.zeros((3, HALO, hps * HEAD_DIM), F32)
        s_ref[...] = jnp.zeros_like(s_ref)

    xbuf[0, HALO:HALO + rows, :] = q_ref[...]
    xbuf[1, HALO:HALO + rows, :] = k_ref[...]
    xbuf[2, HALO:HALO + rows, :] = v_ref[...]

    row = lax.broadcasted_iota(jnp.int32, (C, C), 0)
    col = lax.broadcasted_iota(jnp.int32, (C, C), 1)
    tril = row >= col
    strict = row > col
    eye = (row == col).astype(F32)
    tril_f = tril.astype(F32)
    diag16 = strict & ((row // 16) == (col // 16))
    levels = [strict & ((row // (2 * s)) == (col // (2 * s))) & ((row // s) != (col // s)) for s in (16, 32, 64)]
    lane = lax.broadcasted_iota(jnp.int32, (C, HEAD_DIM), 1)
    neg_a = -jnp.exp(alog_ref[...])
    dtb = dtb_ref[...]
    nw = nw_ref[...]
    cws = (cwq_ref[...], cwk_ref[...], cwv_ref[...])

    def conv(which, r0, sl):
        win = xbuf[which, pl.ds(r0, C + HALO), sl]
        cw = cws[which][:, sl]
        y = win[HALO:] * cw[CONV_WIDTH - 1:CONV_WIDTH, :]
        for back in range(1, CONV_WIDTH):
            shifted = pltpu.roll(win, back, 0)[HALO:]
            y = y + shifted * cw[CONV_WIDTH - 1 - back:CONV_WIDTH - back, :]
        return _silu(y)

    def chunk(c, carry):
        r0 = pl.multiple_of(c * C, C)
        ab = ab_ref[pl.ds(r0, C), :]
        g_all = neg_a * _softplus(ab + dtb)
        beta_all = _sigmoid(ab)
        gcum_all = _dot3(tril_f, g_all)
        gcum_t = gcum_all.T
        for j in range(hps):
            head = hg * hps + j
            sl = slice(j * HEAD_DIM, (j + 1) * HEAD_DIM)
            q = conv(0, r0, sl)
            k = conv(1, r0, sl)
            v = conv(2, r0, sl)
            q = q * lax.rsqrt(jnp.sum(q * q, axis=-1, keepdims=True) + RMS_EPS) * ATTN_SCALE
            k = k * lax.rsqrt(jnp.sum(k * k, axis=-1, keepdims=True) + RMS_EPS)
            gcol = jnp.sum(jnp.where(lane == head, gcum_all, 0.0), axis=1, keepdims=True)
            grow = jnp.sum(jnp.where(row == head, gcum_t, 0.0), axis=0, keepdims=True)
            beta = jnp.sum(jnp.where(lane == GDN_HEADS + head, beta_all, 0.0), axis=1, keepdims=True)
            decay = jnp.where(tril, jnp.exp(jnp.where(tril, gcol - grow, 0.0)), 0.0)
            kb = k * beta
            vb = v * beta
            low = jnp.where(strict, _dot3(kb, k.T) * decay, 0.0)
            a = jnp.where(diag16, -low, 0.0)
            inv = eye + a
            for _ in range(3):
                a = _dot3(a, a)
                inv = inv + _dot3(inv, a)
            for lv in levels:
                off = jnp.where(lv, low, 0.0)
                inv = inv - _dot3(inv, _dot3(off, inv))
            eg = jnp.exp(gcol)
            uw = _dot3(inv, jnp.concatenate([vb, kb * eg], axis=1))
            u = uw[:, :HEAD_DIM]
            w = uw[:, HEAD_DIM:]
            qk = jnp.where(tril, _dot_nt(q.astype(BF16), k.astype(BF16)) * decay, 0.0)
            s = s_ref[j]
            sb = s.astype(BF16)
            v_new = u - _dot(w.astype(BF16), sb)
            vnb = v_new.astype(BF16)
            o = _dot((q * eg).astype(BF16), sb) + _dot(qk.astype(BF16), vnb)
            g_last = gcol[C - 1:C, :]
            kd = k * jnp.exp(g_last - gcol)
            s_ref[j] = s * jnp.exp(g_last) + _dot(kd.T.astype(BF16), vnb)
            z = z_ref[pl.ds(r0, C), sl]
            o_ref[pl.ds(r0, C), sl] = (_rms_rows(o, nw) * _silu(z)).astype(o_ref.dtype)
        return carry

    lax.fori_loop(0, rows // C, chunk, 0)
    xbuf[:, 0:HALO, :] = xbuf[:, rows:rows + HALO, :]


def _gdn(proj, conv_w, a_log, dt_bias, gdn_norm_w, batch, seq):
    hps = 2
    rows = min(512, seq)
    nt = seq // rows
    w = hps * HEAD_DIM
    cwq, cwk, cwv = (conv_w[:, i * GDN_WIDTH:(i + 1) * GDN_WIDTH] for i in range(3))
    pad = lambda t: jnp.zeros((1, 128), F32).at[0, :GDN_HEADS].set(t)

    def col(base):
        return lambda b, g, t: (b * nt + t, base // w + g)

    return pl.pallas_call(
        functools.partial(_gdn_kernel, hps=hps, rows=rows),
        grid=(batch, GDN_HEADS // hps, nt),
        in_specs=[pl.BlockSpec((rows, w), col(COL_GQ)),
                  pl.BlockSpec((rows, w), col(COL_GK)),
                  pl.BlockSpec((rows, w), col(COL_GV)),
                  pl.BlockSpec((rows, w), col(COL_GZ)),
                  pl.BlockSpec((rows, 128), lambda b, g, t: (b * nt + t, COL_AB // 128)),
                  pl.BlockSpec((CONV_WIDTH, w), lambda b, g, t: (0, g)),
                  pl.BlockSpec((CONV_WIDTH, w), lambda b, g, t: (0, g)),
                  pl.BlockSpec((CONV_WIDTH, w), lambda b, g, t: (0, g)),
                  pl.BlockSpec((1, 128), lambda b, g, t: (0, 0)),
                  pl.BlockSpec((1, 128), lambda b, g, t: (0, 0)),
                  pl.BlockSpec((1, 128), lambda b, g, t: (0, 0))],
        out_specs=pl.BlockSpec((rows, w), lambda b, g, t: (b * nt + t, g)),
        out_shape=jax.ShapeDtypeStruct((batch * seq, GDN_WIDTH), BF16),
        scratch_shapes=[pltpu.VMEM((3, rows + HALO, w), F32), pltpu.VMEM((hps, HEAD_DIM, HEAD_DIM), F32)],
        compiler_params=_cparams(("parallel", "parallel", "arbitrary"), 32),
        name="gated_deltanet",
    )(proj, proj, proj, proj, proj, cwq, cwk, cwv, pad(a_log), pad(dt_bias), gdn_norm_w.reshape(1, 128))


def _sb_kernel(q_ref, k_ref, v_ref, nw_ref, o_ref, *, tq, tk):
    qi = pl.program_id(2)
    q0 = qi * tq
    q = q_ref[...].astype(BF16)
    rowi = lax.broadcasted_iota(jnp.int32, (tq, tk), 0)
    coli = lax.broadcasted_iota(jnp.int32, (tq, tk), 1)
    jj = lax.broadcasted_iota(jnp.int32, (tk, tk), 0)
    ss = lax.broadcasted_iota(jnp.int32, (tk, tk), 1)
    later = (jj > ss).astype(BF16)
    nk = (qi + 1) * (tq // tk)

    def body(i, carry):
        acc, run = carry
        k0 = pl.multiple_of((nk - 1 - i) * tk, tk)
        kb = k_ref[pl.ds(k0, tk), :].astype(BF16)
        vb = v_ref[pl.ds(k0, tk), :].astype(BF16)
        z = _dot_nt(q, kb) * ATTN_SCALE
        causal = (coli + k0) < (rowi + q0)
        sp = _softplus(z)
        log_not = jnp.where(causal, -sp, 0.0)
        hi, lo = _split(log_not)
        after = _dot(hi, later) + _dot(lo, later) + run
        a = jnp.where(causal, jnp.exp((z - sp) + after), 0.0)
        acc = acc + _dot(a.astype(BF16), vb)
        run = run + jnp.sum(log_not, axis=1, keepdims=True)
        return acc, run

    acc, _ = lax.fori_loop(0, nk, body, (jnp.zeros((tq, HEAD_DIM), F32), jnp.zeros((tq, 1), F32)))
    o_ref[...] = _rms_rows(acc, nw_ref[...]).astype(o_ref.dtype)


def _sb(proj, sb_norm_w, batch, seq):
    tq = min(256, seq)
    tk = 128
    nq = seq // tq
    cq, ck, cv = COL_SQ // HEAD_DIM, COL_SK // HEAD_DIM, COL_SV // HEAD_DIM
    return pl.pallas_call(
        functools.partial(_sb_kernel, tq=tq, tk=tk),
        grid=(batch, SB_HEADS, nq),
        in_specs=[pl.BlockSpec((tq, HEAD_DIM), lambda b, h, i: (b * nq + i, cq + h)),
                  pl.BlockSpec((seq, HEAD_DIM), lambda b, h, i: (b, ck + h)),
                  pl.BlockSpec((seq, HEAD_DIM), lambda b, h, i: (b, cv + h)),
                  pl.BlockSpec((1, HEAD_DIM), lambda b, h, i: (0, h))],
        out_specs=pl.BlockSpec((tq, HEAD_DIM), lambda b, h, i: (b * nq + i, h)),
        out_shape=jax.ShapeDtypeStruct((batch * seq, SB_WIDTH), BF16),
        compiler_params=_cparams(("parallel", "parallel", "arbitrary"), 32),
        name="stick_breaking",
    )(proj, proj, proj, sb_norm_w.reshape(1, SB_WIDTH))


def _mem_kernel(q_ref, kv_ref, nw_ref, o_ref):
    for h in range(MEM_HEADS):
        sl = slice(h * HEAD_DIM, (h + 1) * HEAD_DIM)
        q = q_ref[:, sl].astype(BF16)
        k = kv_ref[:, sl].astype(BF16)
        v = kv_ref[:, MEM_WIDTH + h * HEAD_DIM:MEM_WIDTH + (h + 1) * HEAD_DIM].astype(BF16)
        s = _dot_nt(q, k) * ATTN_SCALE
        e = jnp.exp(s - jnp.max(s, axis=-1, keepdims=True))
        p = e / jnp.sum(e, axis=-1, keepdims=True)
        o = _dot(p.astype(BF16), v)
        o_ref[:, sl] = _rms_rows(o, nw_ref[:, sl]).astype(o_ref.dtype)


def _mem_attn(proj, kv, mem_norm_w, batch, seq, n_mem):
    tq = min(512, seq)
    nq = seq // tq
    return pl.pallas_call(
        _mem_kernel,
        grid=(batch, nq),
        in_specs=[pl.BlockSpec((tq, MEM_WIDTH), lambda b, i: (b * nq + i, COL_MQ // MEM_WIDTH)),
                  pl.BlockSpec((n_mem, 2 * MEM_WIDTH), lambda b, i: (b, 0)),
                  pl.BlockSpec((1, MEM_WIDTH), lambda b, i: (0, 0))],
        out_specs=pl.BlockSpec((tq, MEM_WIDTH), lambda b, i: (b * nq + i, 0)),
        out_shape=jax.ShapeDtypeStruct((batch * seq, MEM_WIDTH), BF16),
        compiler_params=_cparams(("parallel", "parallel"), 32),
        name="memory_attention",
    )(proj, kv, mem_norm_w.reshape(1, MEM_WIDTH))


def _out_ln_kernel(og_ref, os_ref, om_ref, wg_ref, ws_ref, wm_ref, res_ref, lw_ref, lb_ref, o_ref, ob_ref):
    mix = _dot(og_ref[...], wg_ref[...]) + _dot(os_ref[...], ws_ref[...]) + _dot(om_ref[...], wm_ref[...])
    y = _layer_norm_rows(DEEPNORM_ALPHA * res_ref[...] + mix, lw_ref[...], lb_ref[...])
    o_ref[...] = y
    ob_ref[...] = y.astype(BF16)


def _out_ln(o_gdn, o_sb, o_mem, w_out, res, lw, lb):
    n, d = res.shape
    tm = min(256, n)
    wg, ws, wm = w_out[:GDN_WIDTH], w_out[GDN_WIDTH:GDN_WIDTH + SB_WIDTH], w_out[GDN_WIDTH + SB_WIDTH:]
    rowb = lambda width: pl.BlockSpec((tm, width), lambda i: (i, 0))
    full = lambda r, c: pl.BlockSpec((r, c), lambda i: (0, 0))
    return pl.pallas_call(
        _out_ln_kernel,
        grid=(n // tm,),
        in_specs=[rowb(GDN_WIDTH), rowb(SB_WIDTH), rowb(MEM_WIDTH),
                  full(GDN_WIDTH, d), full(SB_WIDTH, d), full(MEM_WIDTH, d),
                  rowb(d), full(1, d), full(1, d)],
        out_specs=[rowb(d), rowb(d)],
        out_shape=[jax.ShapeDtypeStruct((n, d), F32), jax.ShapeDtypeStruct((n, d), BF16)],
        compiler_params=_cparams(("parallel",), 40),
        name="out_proj_ln",
    )(o_gdn, o_sb, o_mem, wg, ws, wm, res, lw.reshape(1, d), lb.reshape(1, d))


def _peer_score_kernel(h_ref, wq_ref, k1_ref, k2_ref, s1_ref, s2_ref):
    q = _dot(h_ref[...], wq_ref[...]).astype(BF16)
    k1 = k1_ref[...]
    k2 = k2_ref[...]
    for h in range(PEER_HEADS):
        base = h * 2 * PEER_D_HALF
        s1_ref[h] = _dot_nt(k1, q[:, base:base + PEER_D_HALF])
        s2_ref[h] = _dot_nt(k2, q[:, base + PEER_D_HALF:base + 2 * PEER_D_HALF])


def _peer_scores(hb, wq, keys1, keys2):
    n, d = hb.shape
    tb = min(512, n)
    dq = wq.shape[1]
    out = jax.ShapeDtypeStruct((PEER_HEADS, PEER_N_KEYS, n), F32)
    ospec = pl.BlockSpec((PEER_HEADS, PEER_N_KEYS, tb), lambda i: (0, 0, i))
    return pl.pallas_call(
        _peer_score_kernel,
        grid=(n // tb,),
        in_specs=[pl.BlockSpec((tb, d), lambda i: (i, 0)),
                  pl.BlockSpec((d, dq), lambda i: (0, 0)),
                  pl.BlockSpec((PEER_N_KEYS, PEER_D_HALF), lambda i: (0, 0)),
                  pl.BlockSpec((PEER_N_KEYS, PEER_D_HALF), lambda i: (0, 0))],
        out_specs=[ospec, ospec],
        out_shape=[out, out],
        compiler_params=_cparams(("parallel",), 48),
        name="peer_scores",
    )(hb, wq, keys1, keys2)


NTOP = PEER_TOPK + 1
VROWS = 24


def _extract_top(cur, n, rowid, out_ref=None):
    vals = []
    nrows = cur.shape[0]
    for it in range(n):
        m = jnp.max(cur, axis=0, keepdims=True)
        first = jnp.min(jnp.where(cur == m, rowid, float(nrows)), axis=0, keepdims=True)
        cur = jnp.where(rowid == first, NEG_BIG, cur)
        vals.append(m)
        if out_ref is not None:
            out_ref[it:it + 1, :] = m
    return vals


def _peer_gate_kernel(s1_ref, s2_ref, c1_ref, d1_ref, e2_ref, v1_s, v2_s):
    s1 = s1_ref[0]
    s2 = s2_ref[0]
    tb = s1.shape[1]
    rowid = lax.broadcasted_iota(jnp.int32, s1.shape, 0).astype(F32)
    v1_s[...] = jnp.full(v1_s.shape, NEG_BIG, F32)
    v2_s[...] = jnp.full(v2_s.shape, NEG_BIG, F32)
    v1 = _extract_top(s1, NTOP, rowid, v1_s)
    _extract_top(s2, NTOP, rowid, v2_s)
    r8 = lax.broadcasted_iota(jnp.int32, (8, tb), 0)
    pieces = [v1[0] + v2_s[...]]
    for a in range(1, 8):
        nb = NTOP // (a + 1)
        pieces.append(jnp.where(r8 < nb, v1[a] + v2_s[0:8, :], NEG_BIG))
    pieces.append(v1_s[8:VROWS, :] + v2_s[0:1, :])
    cand = jnp.concatenate(pieces, axis=0)
    candid = lax.broadcasted_iota(jnp.int32, cand.shape, 0).astype(F32)
    top = _extract_top(cand, NTOP, candid)
    thr = 0.5 * (top[PEER_TOPK - 1] + top[PEER_TOPK])
    zsum = jnp.sum(jnp.where(cand >= thr, jnp.exp(cand - top[0]), 0.0), axis=0, keepdims=True)
    c1_ref[0] = thr - s1
    d1_ref[0] = jnp.exp(s1 - v1[0]) / zsum
    e2_ref[0] = jnp.exp(s2 - v2_s[0:1, :])


def _peer_gates(s1t, s2t):
    nh, nk, n = s1t.shape
    tb = min(512, n)
    spec = pl.BlockSpec((1, nk, tb), lambda h, i: (h, 0, i))
    out = jax.ShapeDtypeStruct((nh, nk, n), F32)
    return pl.pallas_call(
        _peer_gate_kernel,
        grid=(nh, n // tb),
        in_specs=[spec, spec],
        out_specs=[spec, spec, spec],
        out_shape=[out, out, out],
        scratch_shapes=[pltpu.VMEM((VROWS, tb), F32), pltpu.VMEM((VROWS, tb), F32)],
        compiler_params=_cparams(("parallel", "parallel"), 32),
        name="peer_gates",
    )(s1t, s2t)


SQRT_HALF = 0.7071067811865476
PEER_GROUP = 4
PEER_LANES = 256


def _gelu(x):
    return 0.5 * x * (1.0 + lax.erf(x * SQRT_HALF))


def _peer_mix_kernel(x_ref, u_ref, vt_ref, c1_ref, d1_ref, s2_ref, e2_ref, o_ref, act_s, p_s):
    g = pl.program_id(1)
    tb = x_ref.shape[0]
    nk = PEER_N_KEYS

    @pl.when(g == 0)
    def _():
        o_ref[...] = jnp.zeros_like(o_ref)

    act_s[...] = _dot_nt(u_ref[...], x_ref[...])
    for i in range(PEER_GROUP):
        for c0 in range(0, tb, PEER_LANES):
            cs = slice(c0, c0 + PEER_LANES)
            w = jnp.zeros((nk, PEER_LANES), F32)
            for h in range(PEER_HEADS):
                sel = s2_ref[h, :, cs] >= c1_ref[h, 0, i:i + 1, cs]
                w = w + jnp.where(sel, e2_ref[h, :, cs] * d1_ref[h, 0, i:i + 1, cs], 0.0)
            p_s[i * nk:(i + 1) * nk, cs] = (_gelu(act_s[i * nk:(i + 1) * nk, cs]) * w).astype(BF16)
    o_ref[...] += _dot(vt_ref[...], p_s[...])


def _peer_mix(hb, u_b, vt_b, c1t, d1t, s2t, e2t):
    n, d = hb.shape
    tb = min(1024, n)
    ne = u_b.shape[0]
    eb = PEER_GROUP * PEER_N_KEYS
    ng = ne // eb
    c1g = c1t.reshape(PEER_HEADS, ng, PEER_GROUP, n)
    d1g = d1t.reshape(PEER_HEADS, ng, PEER_GROUP, n)
    gspec = pl.BlockSpec((PEER_HEADS, 1, PEER_GROUP, tb), lambda t, g: (0, g, 0, t))
    kspec = pl.BlockSpec((PEER_HEADS, PEER_N_KEYS, tb), lambda t, g: (0, 0, t))
    return pl.pallas_call(
        _peer_mix_kernel,
        grid=(n // tb, ng),
        in_specs=[pl.BlockSpec((tb, d), lambda t, g: (t, 0)),
                  pl.BlockSpec((eb, d), lambda t, g: (g, 0)),
                  pl.BlockSpec((d, eb), lambda t, g: (0, g)),
                  gspec, gspec, kspec, kspec],
        out_specs=pl.BlockSpec((d, tb), lambda t, g: (0, t)),
        out_shape=jax.ShapeDtypeStruct((d, n), F32),
        scratch_shapes=[pltpu.VMEM((eb, tb), F32), pltpu.VMEM((eb, tb), BF16)],
        compiler_params=_cparams(("parallel", "arbitrary"), 58),
        name="peer_mix",
    )(hb, u_b, vt_b, c1g, d1g, s2t, e2t)


def _final_ln_kernel(h_ref, ft_ref, w_ref, b_ref, o_ref):
    y = DEEPNORM_ALPHA * h_ref[...] + ft_ref[...].T
    o_ref[...] = _layer_norm_rows(y, w_ref[...], b_ref[...])


def _final_ln(h, ffn_t, w, b):
    n, d = h.shape
    tm = min(512, n)
    return pl.pallas_call(
        _final_ln_kernel,
        grid=(n // tm,),
        in_specs=[pl.BlockSpec((tm, d), lambda i: (i, 0)),
                  pl.BlockSpec((d, tm), lambda i: (0, i)),
                  pl.BlockSpec((1, d), lambda i: (0, 0)),
                  pl.BlockSpec((1, d), lambda i: (0, 0))],
        out_specs=pl.BlockSpec((tm, d), lambda i: (i, 0)),
        out_shape=jax.ShapeDtypeStruct((n, d), F32),
        compiler_params=_cparams(("parallel",), 40),
        name="final_ln",
    )(h, ffn_t, w.reshape(1, d), b.reshape(1, d))


def _reorder_w_in(w_in):
    o_ab = 4 * GDN_WIDTH
    d = w_in.shape[0]
    main = jnp.concatenate([w_in[:, :o_ab], w_in[:, o_ab + 2 * GDN_HEADS:]], axis=1)
    ab = jnp.zeros((d, 128), w_in.dtype).at[:, :2 * GDN_HEADS].set(w_in[:, o_ab:o_ab + 2 * GDN_HEADS])
    return jnp.concatenate([main, ab], axis=1)


def kernel(x, mem, ln_emb_w, ln_emb_b, w_in, conv_w, a_log, dt_bias, gdn_norm_w, sb_norm_w,
           mem_norm_w, ln_mem_w, ln_mem_b, w_mem_kv, w_out, ln1_w, ln1_b, peer_wq, peer_keys1,
           peer_keys2, peer_u, peer_v, ln2_w, ln2_b):
    batch, seq, d = x.shape
    n_mem = mem.shape[1]
    l = 0
    h0, h0b = _ln(x.reshape(batch * seq, d), ln_emb_w, ln_emb_b)
    proj = _mm(h0b, _reorder_w_in(w_in[l]).astype(BF16), PROJ_COLS // 7, "in_proj")

    o_gdn = _gdn(proj, conv_w[l], a_log[l], dt_bias[l], gdn_norm_w[l], batch, seq)
    o_sb = _sb(proj, sb_norm_w[l], batch, seq)
    _, mb = _ln(mem.reshape(batch * n_mem, d), ln_mem_w[l], ln_mem_b[l])
    kv = _mm(mb, w_mem_kv[l].astype(BF16), 2 * MEM_WIDTH // 2, "mem_kv")
    o_mem = _mem_attn(proj, kv, mem_norm_w[l], batch, seq, n_mem)

    h1, h1b = _out_ln(o_gdn, o_sb, o_mem, w_out[l].astype(BF16), h0, ln1_w[l], ln1_b[l])

    s1t, s2t = _peer_scores(h1b, peer_wq[l].astype(BF16), peer_keys1[l].astype(BF16), peer_keys2[l].astype(BF16))
    c1t, d1t, e2t = _peer_gates(s1t, s2t)
    ffn_t = _peer_mix(h1b, peer_u[l].astype(BF16), peer_v[l].astype(BF16).T, c1t, d1t, s2t, e2t)
    out = _final_ln(h1, ffn_t, ln2_w[l], ln2_b[l])
    return out.reshape(batch, seq, d)
```

```python
import functools

import jax
import jax.numpy as jnp
from jax import lax
from jax.experimental import pallas as pl
from jax.experimental.pallas import tpu as pltpu

F32 = jnp.float32
BF16 = jnp.bfloat16

HEAD_DIM = 128
GDN_HEADS = 8
SB_HEADS = 4
MEM_HEADS = 4
GDN_WIDTH = GDN_HEADS * HEAD_DIM
SB_WIDTH = SB_HEADS * HEAD_DIM
MEM_WIDTH = MEM_HEADS * HEAD_DIM
CONV_WIDTH = 4
PEER_HEADS = 8
PEER_N_KEYS = 128
PEER_TOPK = 16
PEER_D_HALF = 128
DEPTH = 1
DEEPNORM_ALPHA = (2 * DEPTH) ** 0.25
LN_EPS = 1e-5
RMS_EPS = 1e-6
ATTN_SCALE = HEAD_DIM ** -0.5

COL_GQ = 0
COL_GK = GDN_WIDTH
COL_GV = 2 * GDN_WIDTH
COL_GZ = 3 * GDN_WIDTH
COL_SQ = 4 * GDN_WIDTH
COL_SK = COL_SQ + SB_WIDTH
COL_SV = COL_SK + SB_WIDTH
COL_MQ = COL_SV + SB_WIDTH
COL_AB = COL_MQ + MEM_WIDTH
PROJ_COLS = COL_AB + 128

GDN_CHUNK = 128
HALO = 8
NEG_BIG = -3.0e38
SB_LOG_CUTOFF = -104.0
MIB = 1024 * 1024


def _cparams(sem, vmem_mib):
    return pltpu.CompilerParams(dimension_semantics=sem, vmem_limit_bytes=vmem_mib * MIB)


def _dot(a, b):
    return jnp.dot(a, b, preferred_element_type=F32)


def _dot_nt(a, b):
    return lax.dot_general(a, b, (((1,), (1,)), ((), ())), preferred_element_type=F32)


def _split(a):
    hi = a.astype(BF16)
    lo = (a - hi.astype(F32)).astype(BF16)
    return hi, lo


def _dot3(a, b):
    ah, al = _split(a)
    bh, bl = _split(b)
    return _dot(ah, bh) + (_dot(ah, bl) + _dot(al, bh))


def _dot3_many(as_, bs_):
    sa = [_split(a) for a in as_]
    sb = [_split(b) for b in bs_]
    hh = [_dot(x[0], y[0]) for x, y in zip(sa, sb)]
    hl = [_dot(x[0], y[1]) for x, y in zip(sa, sb)]
    lh = [_dot(x[1], y[0]) for x, y in zip(sa, sb)]
    return [p + (m + n) for p, m, n in zip(hh, hl, lh)]


def _softplus(x):
    return jnp.maximum(x, 0.0) + jnp.log(1.0 + jnp.exp(-jnp.abs(x)))


def _sigmoid(x):
    return 1.0 / (1.0 + jnp.exp(-x))


def _silu(x):
    return x * _sigmoid(x)


def _layer_norm_rows(x, w, b):
    mu = jnp.mean(x, axis=-1, keepdims=True)
    xc = x - mu
    var = jnp.mean(xc * xc, axis=-1, keepdims=True)
    return xc * lax.rsqrt(var + LN_EPS) * w + b


def _rms_rows(x, w):
    return x * lax.rsqrt(jnp.mean(x * x, axis=-1, keepdims=True) + RMS_EPS) * w


def _ln_kernel(x_ref, w_ref, b_ref, o_ref, ob_ref):
    y = _layer_norm_rows(x_ref[...], w_ref[...], b_ref[...])
    o_ref[...] = y
    ob_ref[...] = y.astype(BF16)


def _ln(x2d, w, b):
    n, d = x2d.shape
    tm = min(512, n)
    return pl.pallas_call(
        _ln_kernel,
        grid=(n // tm,),
        in_specs=[pl.BlockSpec((tm, d), lambda i: (i, 0)),
                  pl.BlockSpec((1, d), lambda i: (0, 0)),
                  pl.BlockSpec((1, d), lambda i: (0, 0))],
        out_specs=[pl.BlockSpec((tm, d), lambda i: (i, 0)),
                   pl.BlockSpec((tm, d), lambda i: (i, 0))],
        out_shape=[jax.ShapeDtypeStruct((n, d), F32), jax.ShapeDtypeStruct((n, d), BF16)],
        compiler_params=_cparams(("parallel",), 32),
        name="layer_norm",
    )(x2d, w.reshape(1, d), b.reshape(1, d))


def _mm_kernel(a_ref, w_ref, o_ref):
    o_ref[...] = _dot(a_ref[...], w_ref[...]).astype(o_ref.dtype)


def _mm(a, w, tn, name):
    m, k = a.shape
    n = w.shape[1]
    tm = min(1024, m)
    return pl.pallas_call(
        _mm_kernel,
        grid=(m // tm, n // tn),
        in_specs=[pl.BlockSpec((tm, k), lambda i, j: (i, 0)),
                  pl.BlockSpec((k, tn), lambda i, j: (0, j))],
        out_specs=pl.BlockSpec((tm, tn), lambda i, j: (i, j)),
        out_shape=jax.ShapeDtypeStruct((m, n), F32),
        compiler_params=_cparams(("parallel", "arbitrary"), 40),
        name=name,
    )(a, w)


def _gdn_kernel(q_ref, k_ref, v_ref, z_ref, ab_ref, cwq_ref, cwk_ref, cwv_ref, alog_ref, dtb_ref, nw_ref,
                o_ref, xbuf, s_ref, *, hps, rows):
    C = GDN_CHUNK
    hg = pl.program_id(1)
    tt = pl.program_id(2)

    @pl.when(tt == 0)
    def _():
        xbuf[:, 0:HALO, :] = jnp.zeros((3, HALO, hps * HEAD_DIM), F32)
        s_ref[...] = jnp.zeros_like(s_ref)

    xbuf[0, HALO:HALO + rows, :] = q_ref[...]
    xbuf[1, HALO:HALO + rows, :] = k_ref[...]
    xbuf[2, HALO:HALO + rows, :] = v_ref[...]

    row = lax.broadcasted_iota(jnp.int32, (C, C), 0)
    col = lax.broadcasted_iota(jnp.int32, (C, C), 1)
    tril = row >= col
    strict = row > col
    eye = (row == col).astype(F32)
    tril_f = tril.astype(F32)
    diag16 = strict & ((row // 16) == (col // 16))
    levels = [strict & ((row // (2 * s)) == (col // (2 * s))) & ((row // s) != (col // s)) for s in (16, 32, 64)]
    lane = lax.broadcasted_iota(jnp.int32, (C, HEAD_DIM), 1)
    neg_a = -jnp.exp(alog_ref[...])
    dtb = dtb_ref[...]
    nw = nw_ref[...]
    cws = (cwq_ref[...], cwk_ref[...], cwv_ref[...])

    def conv(which, r0, sl):
        win = xbuf[which, pl.ds(r0, C + HALO), sl]
        cw = cws[which][:, sl]
        y = win[HALO:] * cw[CONV_WIDTH - 1:CONV_WIDTH, :]
        for back in range(1, CONV_WIDTH):
            shifted = pltpu.roll(win, back, 0)[HALO:]
            y = y + shifted * cw[CONV_WIDTH - 1 - back:CONV_WIDTH - back, :]
        return _silu(y)

    def chunk(c, carry):
        r0 = pl.multiple_of(c * C, C)
        ab = ab_ref[pl.ds(r0, C), :]
        g_all = neg_a * _softplus(ab + dtb)
        beta_all = _sigmoid(ab)
        gcum_all = _dot3(tril_f, g_all)
        gcum_t = gcum_all.T
        hs = range(hps)
        sls = [slice(j * HEAD_DIM, (j + 1) * HEAD_DIM) for j in hs]
        heads = [hg * hps + j for j in hs]
        q = [conv(0, r0, sl) for sl in sls]
        k = [conv(1, r0, sl) for sl in sls]
        v = [conv(2, r0, sl) for sl in sls]
        q = [x * lax.rsqrt(jnp.sum(x * x, axis=-1, keepdims=True) + RMS_EPS) * ATTN_SCALE for x in q]
        k = [x * lax.rsqrt(jnp.sum(x * x, axis=-1, keepdims=True) + RMS_EPS) for x in k]
        gcol = [jnp.sum(jnp.where(lane == h, gcum_all, 0.0), axis=1, keepdims=True) for h in heads]
        grow = [jnp.sum(jnp.where(row == h, gcum_t, 0.0), axis=0, keepdims=True) for h in heads]
        beta = [jnp.sum(jnp.where(lane == GDN_HEADS + h, beta_all, 0.0), axis=1, keepdims=True) for h in heads]
        decay = [jnp.where(tril, jnp.exp(jnp.where(tril, gcol[j] - grow[j], 0.0)), 0.0) for j in hs]
        kb = [k[j] * beta[j] for j in hs]
        vb = [v[j] * beta[j] for j in hs]
        kk = _dot3_many(kb, [x.T for x in k])
        low = [jnp.where(strict, kk[j] * decay[j], 0.0) for j in hs]
        a = [jnp.where(diag16, -low[j], 0.0) for j in hs]
        inv = [eye + a[j] for j in hs]
        for _ in range(3):
            a = _dot3_many(a, a)
            t = _dot3_many(inv, a)
            inv = [inv[j] + t[j] for j in hs]
        for lv in levels:
            t = _dot3_many([jnp.where(lv, low[j], 0.0) for j in hs], inv)
            t = _dot3_many(inv, t)
            inv = [inv[j] - t[j] for j in hs]
        eg = [jnp.exp(gcol[j]) for j in hs]
        uw = _dot3_many(inv, [jnp.concatenate([vb[j], kb[j] * eg[j]], axis=1) for j in hs])
        qk = [_dot_nt(q[j].astype(BF16), k[j].astype(BF16)) for j in hs]
        qk = [jnp.where(tril, qk[j] * decay[j], 0.0).astype(BF16) for j in hs]
        s = [s_ref[j] for j in hs]
        sb = [x.astype(BF16) for x in s]
        ws = [_dot(uw[j][:, HEAD_DIM:].astype(BF16), sb[j]) for j in hs]
        o1 = [_dot((q[j] * eg[j]).astype(BF16), sb[j]) for j in hs]
        vnb = [(uw[j][:, :HEAD_DIM] - ws[j]).astype(BF16) for j in hs]
        o2 = [_dot(qk[j], vnb[j]) for j in hs]
        g_last = [gcol[j][C - 1:C, :] for j in hs]
        kdt = [(k[j] * jnp.exp(g_last[j] - gcol[j])).T.astype(BF16) for j in hs]
        upd = [_dot(kdt[j], vnb[j]) for j in hs]
        for j in hs:
            s_ref[j] = s[j] * jnp.exp(g_last[j]) + upd[j]
            z = z_ref[pl.ds(r0, C), sls[j]]
            o_ref[pl.ds(r0, C), sls[j]] = (_rms_rows(o1[j] + o2[j], nw) * _silu(z)).astype(o_ref.dtype)
        return carry

    lax.fori_loop(0, rows // C, chunk, 0)
    xbuf[:, 0:HALO, :] = xbuf[:, rows:rows + HALO, :]


def _gdn(proj, conv_w, a_log, dt_bias, gdn_norm_w, batch, seq):
    hps = 8
    rows = min(512, seq)
    nt = seq // rows
    w = hps * HEAD_DIM
    cwq, cwk, cwv = (conv_w[:, i * GDN_WIDTH:(i + 1) * GDN_WIDTH] for i in range(3))
    pad = lambda t: jnp.zeros((1, 128), F32).at[0, :GDN_HEADS].set(t)

    def col(base):
        return lambda b, g, t: (b * nt + t, base // w + g)

    return pl.pallas_call(
        functools.partial(_gdn_kernel, hps=hps, rows=rows),
        grid=(batch, GDN_HEADS // hps, nt),
        in_specs=[pl.BlockSpec((rows, w), col(COL_GQ)),
                  pl.BlockSpec((rows, w), col(COL_GK)),
                  pl.BlockSpec((rows, w), col(COL_GV)),
                  pl.BlockSpec((rows, w), col(COL_GZ)),
                  pl.BlockSpec((rows, 128), lambda b, g, t: (b * nt + t, COL_AB // 128)),
                  pl.BlockSpec((CONV_WIDTH, w), lambda b, g, t: (0, g)),
                  pl.BlockSpec((CONV_WIDTH, w), lambda b, g, t: (0, g)),
                  pl.BlockSpec((CONV_WIDTH, w), lambda b, g, t: (0, g)),
                  pl.BlockSpec((1, 128), lambda b, g, t: (0, 0)),
                  pl.BlockSpec((1, 128), lambda b, g, t: (0, 0)),
                  pl.BlockSpec((1, 128), lambda b, g, t: (0, 0))],
        out_specs=pl.BlockSpec((rows, w), lambda b, g, t: (b * nt + t, g)),
        out_shape=jax.ShapeDtypeStruct((batch * seq, GDN_WIDTH), BF16),
        scratch_shapes=[pltpu.VMEM((3, rows + HALO, w), F32), pltpu.VMEM((hps, HEAD_DIM, HEAD_DIM), F32)],
        compiler_params=_cparams(("parallel", "parallel", "arbitrary"), 32),
        name="gated_deltanet",
    )(proj, proj, proj, proj, proj, cwq, cwk, cwv, pad(a_log), pad(dt_bias), gdn_norm_w.reshape(1, 128))


def _sb_kernel(q_ref, k_ref, v_ref, nw_ref, o_ref, *, tq, tk):
    qi = pl.program_id(2)
    q0 = qi * tq
    q = q_ref[...].astype(BF16)
    rowi = lax.broadcasted_iota(jnp.int32, (tq, tk), 0)
    coli = lax.broadcasted_iota(jnp.int32, (tq, tk), 1)
    jj = lax.broadcasted_iota(jnp.int32, (tk, tk), 0)
    ss = lax.broadcasted_iota(jnp.int32, (tk, tk), 1)
    later = (jj > ss).astype(BF16)
    nk = (qi + 1) * (tq // tk)

    def cond(carry):
        i, _, _, live = carry
        return jnp.logical_and(i < nk, live > SB_LOG_CUTOFF)

    def body(carry):
        i, acc, run, _ = carry
        k0 = pl.multiple_of((nk - 1 - i) * tk, tk)
        kb = k_ref[pl.ds(k0, tk), :].astype(BF16)
        vb = v_ref[pl.ds(k0, tk), :].astype(BF16)
        z = _dot_nt(q, kb) * ATTN_SCALE
        causal = (coli + k0) < (rowi + q0)
        sp = _softplus(z)
        log_not = jnp.where(causal, -sp, 0.0)
        hi, lo = _split(log_not)
        after = _dot(hi, later) + _dot(lo, later) + run
        a = jnp.where(causal, jnp.exp((z - sp) + after), 0.0)
        acc = acc + _dot(a.astype(BF16), vb)
        run = run + jnp.sum(log_not, axis=1, keepdims=True)
        return i + 1, acc, run, jnp.max(run)

    init = (jnp.int32(0), jnp.zeros((tq, HEAD_DIM), F32), jnp.zeros((tq, 1), F32), jnp.float32(0.0))
    _, acc, _, _ = lax.while_loop(cond, body, init)
    o_ref[...] = _rms_rows(acc, nw_ref[...]).astype(o_ref.dtype)


def _sb(proj, sb_norm_w, batch, seq):
    tq = min(256, seq)
    tk = 128
    nq = seq // tq
    cq, ck, cv = COL_SQ // HEAD_DIM, COL_SK // HEAD_DIM, COL_SV // HEAD_DIM
    return pl.pallas_call(
        functools.partial(_sb_kernel, tq=tq, tk=tk),
        grid=(batch, SB_HEADS, nq),
        in_specs=[pl.BlockSpec((tq, HEAD_DIM), lambda b, h, i: (b * nq + i, cq + h)),
                  pl.BlockSpec((seq, HEAD_DIM), lambda b, h, i: (b, ck + h)),
                  pl.BlockSpec((seq, HEAD_DIM), lambda b, h, i: (b, cv + h)),
                  pl.BlockSpec((1, HEAD_DIM), lambda b, h, i: (0, h))],
        out_specs=pl.BlockSpec((tq, HEAD_DIM), lambda b, h, i: (b * nq + i, h)),
        out_shape=jax.ShapeDtypeStruct((batch * seq, SB_WIDTH), BF16),
        compiler_params=_cparams(("parallel", "parallel", "arbitrary"), 32),
        name="stick_breaking",
    )(proj, proj, proj, sb_norm_w.reshape(1, SB_WIDTH))


def _mem_kernel(q_ref, kv_ref, nw_ref, o_ref):
    for h in range(MEM_HEADS):
        sl = slice(h * HEAD_DIM, (h + 1) * HEAD_DIM)
        q = q_ref[:, sl].astype(BF16)
        k = kv_ref[:, sl].astype(BF16)
        v = kv_ref[:, MEM_WIDTH + h * HEAD_DIM:MEM_WIDTH + (h + 1) * HEAD_DIM].astype(BF16)
        s = _dot_nt(q, k) * ATTN_SCALE
        e = jnp.exp(s - jnp.max(s, axis=-1, keepdims=True))
        p = e / jnp.sum(e, axis=-1, keepdims=True)
        o = _dot(p.astype(BF16), v)
        o_ref[:, sl] = _rms_rows(o, nw_ref[:, sl]).astype(o_ref.dtype)


def _mem_attn(proj, kv, mem_norm_w, batch, seq, n_mem):
    tq = min(512, seq)
    nq = seq // tq
    return pl.pallas_call(
        _mem_kernel,
        grid=(batch, nq),
        in_specs=[pl.BlockSpec((tq, MEM_WIDTH), lambda b, i: (b * nq + i, COL_MQ // MEM_WIDTH)),
                  pl.BlockSpec((n_mem, 2 * MEM_WIDTH), lambda b, i: (b, 0)),
                  pl.BlockSpec((1, MEM_WIDTH), lambda b, i: (0, 0))],
        out_specs=pl.BlockSpec((tq, MEM_WIDTH), lambda b, i: (b * nq + i, 0)),
        out_shape=jax.ShapeDtypeStruct((batch * seq, MEM_WIDTH), BF16),
        compiler_params=_cparams(("parallel", "parallel"), 32),
        name="memory_attention",
    )(proj, kv, mem_norm_w.reshape(1, MEM_WIDTH))


def _out_ln_kernel(og_ref, os_ref, om_ref, wg_ref, ws_ref, wm_ref, res_ref, lw_ref, lb_ref, o_ref, ob_ref):
    mix = _dot(og_ref[...], wg_ref[...]) + _dot(os_ref[...], ws_ref[...]) + _dot(om_ref[...], wm_ref[...])
    y = _layer_norm_rows(DEEPNORM_ALPHA * res_ref[...] + mix, lw_ref[...], lb_ref[...])
    o_ref[...] = y
    ob_ref[...] = y.astype(BF16)


def _out_ln(o_gdn, o_sb, o_mem, w_out, res, lw, lb):
    n, d = res.shape
    tm = min(256, n)
    wg, ws, wm = w_out[:GDN_WIDTH], w_out[GDN_WIDTH:GDN_WIDTH + SB_WIDTH], w_out[GDN_WIDTH + SB_WIDTH:]
    rowb = lambda width: pl.BlockSpec((tm, width), lambda i: (i, 0))
    full = lambda r, c: pl.BlockSpec((r, c), lambda i: (0, 0))
    return pl.pallas_call(
        _out_ln_kernel,
        grid=(n // tm,),
        in_specs=[rowb(GDN_WIDTH), rowb(SB_WIDTH), rowb(MEM_WIDTH),
                  full(GDN_WIDTH, d), full(SB_WIDTH, d), full(MEM_WIDTH, d),
                  rowb(d), full(1, d), full(1, d)],
        out_specs=[rowb(d), rowb(d)],
        out_shape=[jax.ShapeDtypeStruct((n, d), F32), jax.ShapeDtypeStruct((n, d), BF16)],
        compiler_params=_cparams(("parallel",), 40),
        name="out_proj_ln",
    )(o_gdn, o_sb, o_mem, wg, ws, wm, res, lw.reshape(1, d), lb.reshape(1, d))


def _peer_score_kernel(h_ref, wq_ref, k1_ref, k2_ref, s1_ref, s2_ref):
    q = _dot(h_ref[...], wq_ref[...]).astype(BF16)
    k1 = k1_ref[...]
    k2 = k2_ref[...]
    for h in range(PEER_HEADS):
        base = h * 2 * PEER_D_HALF
        s1_ref[h] = _dot_nt(k1, q[:, base:base + PEER_D_HALF])
        s2_ref[h] = _dot_nt(k2, q[:, base + PEER_D_HALF:base + 2 * PEER_D_HALF])


def _peer_scores(hb, wq, keys1, keys2):
    n, d = hb.shape
    tb = min(512, n)
    dq = wq.shape[1]
    out = jax.ShapeDtypeStruct((PEER_HEADS, PEER_N_KEYS, n), F32)
    ospec = pl.BlockSpec((PEER_HEADS, PEER_N_KEYS, tb), lambda i: (0, 0, i))
    return pl.pallas_call(
        _peer_score_kernel,
        grid=(n // tb,),
        in_specs=[pl.BlockSpec((tb, d), lambda i: (i, 0)),
                  pl.BlockSpec((d, dq), lambda i: (0, 0)),
                  pl.BlockSpec((PEER_N_KEYS, PEER_D_HALF), lambda i: (0, 0)),
                  pl.BlockSpec((PEER_N_KEYS, PEER_D_HALF), lambda i: (0, 0))],
        out_specs=[ospec, ospec],
        out_shape=[out, out],
        compiler_params=_cparams(("parallel",), 48),
        name="peer_scores",
    )(hb, wq, keys1, keys2)


NTOP = PEER_TOPK + 1
VROWS = 24


def _extract_top(cur, n, rowid, out_ref=None):
    vals = []
    nrows = cur.shape[0]
    for it in range(n):
        m = jnp.max(cur, axis=0, keepdims=True)
        first = jnp.min(jnp.where(cur == m, rowid, float(nrows)), axis=0, keepdims=True)
        cur = jnp.where(rowid == first, NEG_BIG, cur)
        vals.append(m)
        if out_ref is not None:
            out_ref[it:it + 1, :] = m
    return vals


def _peer_gate_kernel(s1_ref, s2_ref, t1_ref, d1_ref, e2_ref, v1_s, v2_s):
    s1 = s1_ref[0]
    s2 = s2_ref[0]
    tb = s1.shape[1]
    rowid = lax.broadcasted_iota(jnp.int32, s1.shape, 0).astype(F32)
    v1_s[...] = jnp.full(v1_s.shape, NEG_BIG, F32)
    v2_s[...] = jnp.full(v2_s.shape, NEG_BIG, F32)
    v1 = _extract_top(s1, NTOP, rowid, v1_s)
    _extract_top(s2, NTOP, rowid, v2_s)
    r8 = lax.broadcasted_iota(jnp.int32, (8, tb), 0)
    pieces = [v1[0] + v2_s[...]]
    for a in range(1, 8):
        nb = NTOP // (a + 1)
        pieces.append(jnp.where(r8 < nb, v1[a] + v2_s[0:8, :], NEG_BIG))
    pieces.append(v1_s[8:VROWS, :] + v2_s[0:1, :])
    cand = jnp.concatenate(pieces, axis=0)
    candid = lax.broadcasted_iota(jnp.int32, cand.shape, 0).astype(F32)
    top = _extract_top(cand, NTOP, candid)
    thr = 0.5 * (top[PEER_TOPK - 1] + top[PEER_TOPK])
    zsum = jnp.sum(jnp.where(cand >= thr, jnp.exp(cand - top[0]), 0.0), axis=0, keepdims=True)
    m2 = v2_s[0:1, :]
    t1_ref[0] = jnp.exp((thr - s1) - m2)
    d1_ref[0] = jnp.exp(s1 - v1[0]) / zsum
    e2_ref[0] = jnp.exp(s2 - m2)


def _peer_gates(s1t, s2t):
    nh, nk, n = s1t.shape
    tb = min(512, n)
    spec = pl.BlockSpec((1, nk, tb), lambda h, i: (h, 0, i))
    out = jax.ShapeDtypeStruct((nh, nk, n), F32)
    return pl.pallas_call(
        _peer_gate_kernel,
        grid=(nh, n // tb),
        in_specs=[spec, spec],
        out_specs=[spec, spec, spec],
        out_shape=[out, out, out],
        scratch_shapes=[pltpu.VMEM((VROWS, tb), F32), pltpu.VMEM((VROWS, tb), F32)],
        compiler_params=_cparams(("parallel", "parallel"), 32),
        name="peer_gates",
    )(s1t, s2t)


SQRT_HALF = 0.7071067811865476
PEER_GROUP = 8
PEER_LANES = 256


def _gelu(x):
    return 0.5 * x * (1.0 + lax.erf(x * SQRT_HALF))


def _peer_mix_kernel(x_ref, u_ref, vt_ref, t1_ref, d1_ref, e2_ref, o_ref, *scratch):
    nchunk = len(scratch) // 2
    act_refs = scratch[:nchunk]
    p_refs = scratch[nchunk:]
    g = pl.program_id(1)
    nk = PEER_N_KEYS

    @pl.when(g == 0)
    def _():
        o_ref[...] = jnp.zeros_like(o_ref)

    def cols(c):
        return slice(c * PEER_LANES, (c + 1) * PEER_LANES)

    def scores(c):
        act_refs[c][...] = _dot_nt(u_ref[...], x_ref[cols(c), :])

    def weigh(c):
        cs = cols(c)
        for i in range(PEER_GROUP):
            rs = slice(i * nk, (i + 1) * nk)
            w = jnp.zeros((nk, PEER_LANES), F32)
            for h in range(PEER_HEADS):
                e2 = e2_ref[h, :, cs]
                w = w + jnp.where(e2 >= t1_ref[h, i:i + 1, cs], e2, 0.0) * d1_ref[h, i:i + 1, cs]
            p_refs[c][rs, :] = (_gelu(act_refs[c][rs, :]) * w).astype(BF16)

    def mix(c):
        o_ref[:, cols(c)] += _dot(vt_ref[...], p_refs[c][...])

    scores(0)
    for c in range(nchunk):
        if c + 1 < nchunk:
            scores(c + 1)
        weigh(c)
        if c >= 1:
            mix(c - 1)
    mix(nchunk - 1)


def _peer_mix(hb, u_b, vt_b, t1t, d1t, e2t):
    n, d = hb.shape
    tb = min(1024, n)
    ne = u_b.shape[0]
    eb = PEER_GROUP * PEER_N_KEYS
    ng = ne // eb
    gspec = pl.BlockSpec((PEER_HEADS, PEER_GROUP, tb), lambda t, g: (0, g, t))
    kspec = pl.BlockSpec((PEER_HEADS, PEER_N_KEYS, tb), lambda t, g: (0, 0, t))
    nchunk = tb // PEER_LANES
    return pl.pallas_call(
        _peer_mix_kernel,
        grid=(n // tb, ng),
        in_specs=[pl.BlockSpec((tb, d), lambda t, g: (t, 0)),
                  pl.BlockSpec((eb, d), lambda t, g: (g, 0)),
                  pl.BlockSpec((d, eb), lambda t, g: (0, g)),
                  gspec, gspec, kspec],
        out_specs=pl.BlockSpec((d, tb), lambda t, g: (0, t)),
        out_shape=jax.ShapeDtypeStruct((d, n), F32),
        scratch_shapes=([pltpu.VMEM((eb, PEER_LANES), F32)] * nchunk
                        + [pltpu.VMEM((eb, PEER_LANES), BF16)] * nchunk),
        compiler_params=_cparams(("parallel", "arbitrary"), 60),
        name="peer_mix",
    )(hb, u_b, vt_b, t1t, d1t, e2t)


def _final_ln_kernel(h_ref, ft_ref, w_ref, b_ref, o_ref):
    y = DEEPNORM_ALPHA * h_ref[...] + ft_ref[...].T
    o_ref[...] = _layer_norm_rows(y, w_ref[...], b_ref[...])


def _final_ln(h, ffn_t, w, b):
    n, d = h.shape
    tm = min(512, n)
    return pl.pallas_call(
        _final_ln_kernel,
        grid=(n // tm,),
        in_specs=[pl.BlockSpec((tm, d), lambda i: (i, 0)),
                  pl.BlockSpec((d, tm), lambda i: (0, i)),
                  pl.BlockSpec((1, d), lambda i: (0, 0)),
                  pl.BlockSpec((1, d), lambda i: (0, 0))],
        out_specs=pl.BlockSpec((tm, d), lambda i: (i, 0)),
        out_shape=jax.ShapeDtypeStruct((n, d), F32),
        compiler_params=_cparams(("parallel",), 40),
        name="final_ln",
    )(h, ffn_t, w.reshape(1, d), b.reshape(1, d))


def _reorder_w_in(w_in):
    o_ab = 4 * GDN_WIDTH
    d = w_in.shape[0]
    main = jnp.concatenate([w_in[:, :o_ab], w_in[:, o_ab + 2 * GDN_HEADS:]], axis=1)
    ab = jnp.zeros((d, 128), w_in.dtype).at[:, :2 * GDN_HEADS].set(w_in[:, o_ab:o_ab + 2 * GDN_HEADS])
    return jnp.concatenate([main, ab], axis=1)


def kernel(x, mem, ln_emb_w, ln_emb_b, w_in, conv_w, a_log, dt_bias, gdn_norm_w, sb_norm_w,
           mem_norm_w, ln_mem_w, ln_mem_b, w_mem_kv, w_out, ln1_w, ln1_b, peer_wq, peer_keys1,
           peer_keys2, peer_u, peer_v, ln2_w, ln2_b):
    batch, seq, d = x.shape
    n_mem = mem.shape[1]
    l = 0
    h0, h0b = _ln(x.reshape(batch * seq, d), ln_emb_w, ln_emb_b)
    proj = _mm(h0b, _reorder_w_in(w_in[l]).astype(BF16), PROJ_COLS // 7, "in_proj")

    o_gdn = _gdn(proj, conv_w[l], a_log[l], dt_bias[l], gdn_norm_w[l], batch, seq)
    o_sb = _sb(proj, sb_norm_w[l], batch, seq)
    _, mb = _ln(mem.reshape(batch * n_mem, d), ln_mem_w[l], ln_mem_b[l])
    kv = _mm(mb, w_mem_kv[l].astype(BF16), 2 * MEM_WIDTH // 2, "mem_kv")
    o_mem = _mem_attn(proj, kv, mem_norm_w[l], batch, seq, n_mem)

    h1, h1b = _out_ln(o_gdn, o_sb, o_mem, w_out[l].astype(BF16), h0, ln1_w[l], ln1_b[l])

    s1t, s2t = _peer_scores(h1b, peer_wq[l].astype(BF16), peer_keys1[l].astype(BF16), peer_keys2[l].astype(BF16))
    t1t, d1t, e2t = _peer_gates(s1t, s2t)
    ffn_t = _peer_mix(h1b, peer_u[l].astype(BF16), peer_v[l].astype(BF16).T, t1t, d1t, e2t)
    out = _final_ln(h1, ffn_t, ln2_w[l], ln2_b[l])
    return out.reshape(batch, seq, d)
```

```python
import functools

import jax
import jax.numpy as jnp
from jax import lax
from jax.experimental import pallas as pl
from jax.experimental.pallas import tpu as pltpu

F32 = jnp.float32
BF16 = jnp.bfloat16

HEAD_DIM = 128
GDN_HEADS = 8
SB_HEADS = 4
MEM_HEADS = 4
GDN_WIDTH = GDN_HEADS * HEAD_DIM
SB_WIDTH = SB_HEADS * HEAD_DIM
MEM_WIDTH = MEM_HEADS * HEAD_DIM
CONV_WIDTH = 4
PEER_HEADS = 8
PEER_N_KEYS = 128
PEER_TOPK = 16
PEER_D_HALF = 128
DEPTH = 1
DEEPNORM_ALPHA = (2 * DEPTH) ** 0.25
LN_EPS = 1e-5
RMS_EPS = 1e-6
ATTN_SCALE = HEAD_DIM ** -0.5

COL_GQ = 0
COL_GK = GDN_WIDTH
COL_GV = 2 * GDN_WIDTH
COL_GZ = 3 * GDN_WIDTH
COL_SQ = 4 * GDN_WIDTH
COL_SK = COL_SQ + SB_WIDTH
COL_SV = COL_SK + SB_WIDTH
COL_MQ = COL_SV + SB_WIDTH
COL_AB = COL_MQ + MEM_WIDTH
PROJ_COLS = COL_AB + 128

GDN_CHUNK = 128
HALO = 8
NEG_BIG = -3.0e38
SB_LOG_CUTOFF = -104.0
MIB = 1024 * 1024


def _cparams(sem, vmem_mib):
    return pltpu.CompilerParams(dimension_semantics=sem, vmem_limit_bytes=vmem_mib * MIB)


def _dot(a, b):
    return jnp.dot(a, b, preferred_element_type=F32)


def _dot_nt(a, b):
    return lax.dot_general(a, b, (((1,), (1,)), ((), ())), preferred_element_type=F32)


def _split(a):
    hi = a.astype(BF16)
    lo = (a - hi.astype(F32)).astype(BF16)
    return hi, lo


def _dot3(a, b):
    ah, al = _split(a)
    bh, bl = _split(b)
    return _dot(ah, bh) + (_dot(ah, bl) + _dot(al, bh))


def _dot3_many(as_, bs_):
    sa = [_split(a) for a in as_]
    sb = [_split(b) for b in bs_]
    hh = [_dot(x[0], y[0]) for x, y in zip(sa, sb)]
    hl = [_dot(x[0], y[1]) for x, y in zip(sa, sb)]
    lh = [_dot(x[1], y[0]) for x, y in zip(sa, sb)]
    return [p + (m + n) for p, m, n in zip(hh, hl, lh)]


def _softplus(x):
    return jnp.maximum(x, 0.0) + jnp.log(1.0 + jnp.exp(-jnp.abs(x)))


def _sigmoid(x):
    return 1.0 / (1.0 + jnp.exp(-x))


def _silu(x):
    return x * _sigmoid(x)


def _layer_norm_rows(x, w, b):
    mu = jnp.mean(x, axis=-1, keepdims=True)
    xc = x - mu
    var = jnp.mean(xc * xc, axis=-1, keepdims=True)
    return xc * lax.rsqrt(var + LN_EPS) * w + b


def _rms_rows(x, w):
    return x * lax.rsqrt(jnp.mean(x * x, axis=-1, keepdims=True) + RMS_EPS) * w


def _ln_kernel(x_ref, w_ref, b_ref, o_ref, ob_ref):
    y = _layer_norm_rows(x_ref[...], w_ref[...], b_ref[...])
    o_ref[...] = y
    ob_ref[...] = y.astype(BF16)


def _ln(x2d, w, b):
    n, d = x2d.shape
    tm = min(512, n)
    return pl.pallas_call(
        _ln_kernel,
        grid=(n // tm,),
        in_specs=[pl.BlockSpec((tm, d), lambda i: (i, 0)),
                  pl.BlockSpec((1, d), lambda i: (0, 0)),
                  pl.BlockSpec((1, d), lambda i: (0, 0))],
        out_specs=[pl.BlockSpec((tm, d), lambda i: (i, 0)),
                   pl.BlockSpec((tm, d), lambda i: (i, 0))],
        out_shape=[jax.ShapeDtypeStruct((n, d), F32), jax.ShapeDtypeStruct((n, d), BF16)],
        compiler_params=_cparams(("parallel",), 32),
        name="layer_norm",
    )(x2d, w.reshape(1, d), b.reshape(1, d))


def _mm_kernel(a_ref, w_ref, o_ref):
    o_ref[...] = _dot(a_ref[...], w_ref[...]).astype(o_ref.dtype)


def _mm(a, w, tn, name):
    m, k = a.shape
    n = w.shape[1]
    tm = min(1024, m)
    return pl.pallas_call(
        _mm_kernel,
        grid=(m // tm, n // tn),
        in_specs=[pl.BlockSpec((tm, k), lambda i, j: (i, 0)),
                  pl.BlockSpec((k, tn), lambda i, j: (0, j))],
        out_specs=pl.BlockSpec((tm, tn), lambda i, j: (i, j)),
        out_shape=jax.ShapeDtypeStruct((m, n), F32),
        compiler_params=_cparams(("parallel", "arbitrary"), 40),
        name=name,
    )(a, w)


def _gdn_kernel(q_ref, k_ref, v_ref, z_ref, ab_ref, cwq_ref, cwk_ref, cwv_ref, alog_ref, dtb_ref, nw_ref,
                o_ref, xbuf, s_ref, *, hps, rows):
    C = GDN_CHUNK
    hg = pl.program_id(1)
    tt = pl.program_id(2)

    @pl.when(tt == 0)
    def _():
        xbuf[:, 0:HALO, :] = jnp.zeros((3, HALO, hps * HEAD_DIM), F32)
        s_ref[...] = jnp.zeros_like(s_ref)

    xbuf[0, HALO:HALO + rows, :] = q_ref[...]
    xbuf[1, HALO:HALO + rows, :] = k_ref[...]
    xbuf[2, HALO:HALO + rows, :] = v_ref[...]

    row = lax.broadcasted_iota(jnp.int32, (C, C), 0)
    col = lax.broadcasted_iota(jnp.int32, (C, C), 1)
    tril = row >= col
    strict = row > col
    eye = (row == col).astype(F32)
    tril_f = tril.astype(F32)
    diag16 = strict & ((row // 16) == (col // 16))
    levels = [strict & ((row // (2 * s)) == (col // (2 * s))) & ((row // s) != (col // s)) for s in (16, 32, 64)]
    lane = lax.broadcasted_iota(jnp.int32, (C, HEAD_DIM), 1)
    neg_a = -jnp.exp(alog_ref[...])
    dtb = dtb_ref[...]
    nw = nw_ref[...]
    cws = (cwq_ref[...], cwk_ref[...], cwv_ref[...])

    def conv(which, r0, sl):
        win = xbuf[which, pl.ds(r0, C + HALO), sl]
        cw = cws[which][:, sl]
        y = win[HALO:] * cw[CONV_WIDTH - 1:CONV_WIDTH, :]
        for back in range(1, CONV_WIDTH):
            shifted = pltpu.roll(win, back, 0)[HALO:]
            y = y + shifted * cw[CONV_WIDTH - 1 - back:CONV_WIDTH - back, :]
        return _silu(y)

    def chunk(c, carry):
        r0 = pl.multiple_of(c * C, C)
        ab = ab_ref[pl.ds(r0, C), :]
        g_all = neg_a * _softplus(ab + dtb)
        beta_all = _sigmoid(ab)
        gcum_all = _dot3(tril_f, g_all)
        gcum_t = gcum_all.T
        hs = range(hps)
        sls = [slice(j * HEAD_DIM, (j + 1) * HEAD_DIM) for j in hs]
        heads = [hg * hps + j for j in hs]
        q = [conv(0, r0, sl) for sl in sls]
        k = [conv(1, r0, sl) for sl in sls]
        v = [conv(2, r0, sl) for sl in sls]
        q = [x * lax.rsqrt(jnp.sum(x * x, axis=-1, keepdims=True) + RMS_EPS) * ATTN_SCALE for x in q]
        k = [x * lax.rsqrt(jnp.sum(x * x, axis=-1, keepdims=True) + RMS_EPS) for x in k]
        gcol = [jnp.sum(jnp.where(lane == h, gcum_all, 0.0), axis=1, keepdims=True) for h in heads]
        grow = [jnp.sum(jnp.where(row == h, gcum_t, 0.0), axis=0, keepdims=True) for h in heads]
        beta = [jnp.sum(jnp.where(lane == GDN_HEADS + h, beta_all, 0.0), axis=1, keepdims=True) for h in heads]
        decay = [jnp.where(tril, jnp.exp(jnp.where(tril, gcol[j] - grow[j], 0.0)), 0.0) for j in hs]
        kb = [k[j] * beta[j] for j in hs]
        vb = [v[j] * beta[j] for j in hs]
        kk = _dot3_many(kb, [x.T for x in k])
        low = [jnp.where(strict, kk[j] * decay[j], 0.0) for j in hs]
        a = [jnp.where(diag16, -low[j], 0.0) for j in hs]
        inv = [eye + a[j] for j in hs]
        for _ in range(3):
            a = _dot3_many(a, a)
            t = _dot3_many(inv, a)
            inv = [inv[j] + t[j] for j in hs]
        for lv in levels:
            t = _dot3_many([jnp.where(lv, low[j], 0.0) for j in hs], inv)
            t = _dot3_many(inv, t)
            inv = [inv[j] - t[j] for j in hs]
        eg = [jnp.exp(gcol[j]) for j in hs]
        uw = _dot3_many(inv, [jnp.concatenate([vb[j], kb[j] * eg[j]], axis=1) for j in hs])
        qk = [_dot_nt(q[j].astype(BF16), k[j].astype(BF16)) for j in hs]
        qk = [jnp.where(tril, qk[j] * decay[j], 0.0).astype(BF16) for j in hs]
        s = [s_ref[j] for j in hs]
        sb = [x.astype(BF16) for x in s]
        ws = [_dot(uw[j][:, HEAD_DIM:].astype(BF16), sb[j]) for j in hs]
        o1 = [_dot((q[j] * eg[j]).astype(BF16), sb[j]) for j in hs]
        vnb = [(uw[j][:, :HEAD_DIM] - ws[j]).astype(BF16) for j in hs]
        o2 = [_dot(qk[j], vnb[j]) for j in hs]
        g_last = [gcol[j][C - 1:C, :] for j in hs]
        kdt = [(k[j] * jnp.exp(g_last[j] - gcol[j])).T.astype(BF16) for j in hs]
        upd = [_dot(kdt[j], vnb[j]) for j in hs]
        for j in hs:
            s_ref[j] = s[j] * jnp.exp(g_last[j]) + upd[j]
            z = z_ref[pl.ds(r0, C), sls[j]]
            o_ref[pl.ds(r0, C), sls[j]] = (_rms_rows(o1[j] + o2[j], nw) * _silu(z)).astype(o_ref.dtype)
        return carry

    lax.fori_loop(0, rows // C, chunk, 0)
    xbuf[:, 0:HALO, :] = xbuf[:, rows:rows + HALO, :]


def _gdn(proj, conv_w, a_log, dt_bias, gdn_norm_w, batch, seq):
    hps = 8
    rows = min(512, seq)
    nt = seq // rows
    w = hps * HEAD_DIM
    cwq, cwk, cwv = (conv_w[:, i * GDN_WIDTH:(i + 1) * GDN_WIDTH] for i in range(3))
    pad = lambda t: jnp.zeros((1, 128), F32).at[0, :GDN_HEADS].set(t)

    def col(base):
        return lambda b, g, t: (b * nt + t, base // w + g)

    return pl.pallas_call(
        functools.partial(_gdn_kernel, hps=hps, rows=rows),
        grid=(batch, GDN_HEADS // hps, nt),
        in_specs=[pl.BlockSpec((rows, w), col(COL_GQ)),
                  pl.BlockSpec((rows, w), col(COL_GK)),
                  pl.BlockSpec((rows, w), col(COL_GV)),
                  pl.BlockSpec((rows, w), col(COL_GZ)),
                  pl.BlockSpec((rows, 128), lambda b, g, t: (b * nt + t, COL_AB // 128)),
                  pl.BlockSpec((CONV_WIDTH, w), lambda b, g, t: (0, g)),
                  pl.BlockSpec((CONV_WIDTH, w), lambda b, g, t: (0, g)),
                  pl.BlockSpec((CONV_WIDTH, w), lambda b, g, t: (0, g)),
                  pl.BlockSpec((1, 128), lambda b, g, t: (0, 0)),
                  pl.BlockSpec((1, 128), lambda b, g, t: (0, 0)),
                  pl.BlockSpec((1, 128), lambda b, g, t: (0, 0))],
        out_specs=pl.BlockSpec((rows, w), lambda b, g, t: (b * nt + t, g)),
        out_shape=jax.ShapeDtypeStruct((batch * seq, GDN_WIDTH), BF16),
        scratch_shapes=[pltpu.VMEM((3, rows + HALO, w), F32), pltpu.VMEM((hps, HEAD_DIM, HEAD_DIM), F32)],
        compiler_params=_cparams(("parallel", "parallel", "arbitrary"), 32),
        name="gated_deltanet",
    )(proj, proj, proj, proj, proj, cwq, cwk, cwv, pad(a_log), pad(dt_bias), gdn_norm_w.reshape(1, 128))


def _sb_kernel(q_ref, k_ref, v_ref, nw_ref, o_ref, *, tq, tk, hps):
    qi = pl.program_id(2)
    q0 = qi * tq
    hs = range(hps)
    sls = [slice(j * HEAD_DIM, (j + 1) * HEAD_DIM) for j in hs]
    q = [q_ref[:, sl].astype(BF16) for sl in sls]
    rowi = lax.broadcasted_iota(jnp.int32, (tq, tk), 0)
    coli = lax.broadcasted_iota(jnp.int32, (tq, tk), 1)
    jj = lax.broadcasted_iota(jnp.int32, (tk, tk), 0)
    ss = lax.broadcasted_iota(jnp.int32, (tk, tk), 1)
    later = (jj > ss).astype(BF16)
    nk = (qi + 1) * (tq // tk)

    def cond(carry):
        i, _, _, live = carry
        return jnp.logical_and(i < nk, live > SB_LOG_CUTOFF)

    def body(carry):
        i, acc, run, _ = carry
        k0 = pl.multiple_of((nk - 1 - i) * tk, tk)
        causal = (coli + k0) < (rowi + q0)
        kb = [k_ref[pl.ds(k0, tk), sl].astype(BF16) for sl in sls]
        vb = [v_ref[pl.ds(k0, tk), sl].astype(BF16) for sl in sls]
        z = [_dot_nt(q[j], kb[j]) * ATTN_SCALE for j in hs]
        sp = [_softplus(z[j]) for j in hs]
        log_not = [jnp.where(causal, -sp[j], 0.0) for j in hs]
        parts = [_split(log_not[j]) for j in hs]
        after = [_dot(parts[j][0], later) + _dot(parts[j][1], later) + run[j] for j in hs]
        a = [jnp.where(causal, jnp.exp((z[j] - sp[j]) + after[j]), 0.0).astype(BF16) for j in hs]
        acc = tuple(acc[j] + _dot(a[j], vb[j]) for j in hs)
        run = tuple(run[j] + jnp.sum(log_not[j], axis=1, keepdims=True) for j in hs)
        live = jnp.max(run[0])
        for j in range(1, hps):
            live = jnp.maximum(live, jnp.max(run[j]))
        return i + 1, acc, run, live

    init = (jnp.int32(0), tuple(jnp.zeros((tq, HEAD_DIM), F32) for _ in hs),
            tuple(jnp.zeros((tq, 1), F32) for _ in hs), jnp.float32(0.0))
    _, acc, _, _ = lax.while_loop(cond, body, init)
    for j in hs:
        o_ref[:, sls[j]] = _rms_rows(acc[j], nw_ref[:, sls[j]]).astype(o_ref.dtype)


def _sb(proj, sb_norm_w, batch, seq):
    tq = min(256, seq)
    tk = 128
    hps = 4
    w = hps * HEAD_DIM
    nq = seq // tq
    cq, ck, cv = COL_SQ // w, COL_SK // w, COL_SV // w
    return pl.pallas_call(
        functools.partial(_sb_kernel, tq=tq, tk=tk, hps=hps),
        grid=(batch, SB_HEADS // hps, nq),
        in_specs=[pl.BlockSpec((tq, w), lambda b, h, i: (b * nq + i, cq + h)),
                  pl.BlockSpec((seq, w), lambda b, h, i: (b, ck + h)),
                  pl.BlockSpec((seq, w), lambda b, h, i: (b, cv + h)),
                  pl.BlockSpec((1, w), lambda b, h, i: (0, h))],
        out_specs=pl.BlockSpec((tq, w), lambda b, h, i: (b * nq + i, h)),
        out_shape=jax.ShapeDtypeStruct((batch * seq, SB_WIDTH), BF16),
        compiler_params=_cparams(("parallel", "parallel", "arbitrary"), 48),
        name="stick_breaking",
    )(proj, proj, proj, sb_norm_w.reshape(1, SB_WIDTH))


def _mem_kernel(q_ref, kv_ref, nw_ref, o_ref):
    for h in range(MEM_HEADS):
        sl = slice(h * HEAD_DIM, (h + 1) * HEAD_DIM)
        q = q_ref[:, sl].astype(BF16)
        k = kv_ref[:, sl].astype(BF16)
        v = kv_ref[:, MEM_WIDTH + h * HEAD_DIM:MEM_WIDTH + (h + 1) * HEAD_DIM].astype(BF16)
        s = _dot_nt(q, k) * ATTN_SCALE
        e = jnp.exp(s - jnp.max(s, axis=-1, keepdims=True))
        p = e / jnp.sum(e, axis=-1, keepdims=True)
        o = _dot(p.astype(BF16), v)
        o_ref[:, sl] = _rms_rows(o, nw_ref[:, sl]).astype(o_ref.dtype)


def _mem_attn(proj, kv, mem_norm_w, batch, seq, n_mem):
    tq = min(512, seq)
    nq = seq // tq
    return pl.pallas_call(
        _mem_kernel,
        grid=(batch, nq),
        in_specs=[pl.BlockSpec((tq, MEM_WIDTH), lambda b, i: (b * nq + i, COL_MQ // MEM_WIDTH)),
                  pl.BlockSpec((n_mem, 2 * MEM_WIDTH), lambda b, i: (b, 0)),
                  pl.BlockSpec((1, MEM_WIDTH), lambda b, i: (0, 0))],
        out_specs=pl.BlockSpec((tq, MEM_WIDTH), lambda b, i: (b * nq + i, 0)),
        out_shape=jax.ShapeDtypeStruct((batch * seq, MEM_WIDTH), BF16),
        compiler_params=_cparams(("parallel", "parallel"), 32),
        name="memory_attention",
    )(proj, kv, mem_norm_w.reshape(1, MEM_WIDTH))


def _out_ln_kernel(og_ref, os_ref, om_ref, wg_ref, ws_ref, wm_ref, res_ref, lw_ref, lb_ref, o_ref, ob_ref):
    mix = _dot(og_ref[...], wg_ref[...]) + _dot(os_ref[...], ws_ref[...]) + _dot(om_ref[...], wm_ref[...])
    y = _layer_norm_rows(DEEPNORM_ALPHA * res_ref[...] + mix, lw_ref[...], lb_ref[...])
    o_ref[...] = y
    ob_ref[...] = y.astype(BF16)


def _out_ln(o_gdn, o_sb, o_mem, w_out, res, lw, lb):
    n, d = res.shape
    tm = min(256, n)
    wg, ws, wm = w_out[:GDN_WIDTH], w_out[GDN_WIDTH:GDN_WIDTH + SB_WIDTH], w_out[GDN_WIDTH + SB_WIDTH:]
    rowb = lambda width: pl.BlockSpec((tm, width), lambda i: (i, 0))
    full = lambda r, c: pl.BlockSpec((r, c), lambda i: (0, 0))
    return pl.pallas_call(
        _out_ln_kernel,
        grid=(n // tm,),
        in_specs=[rowb(GDN_WIDTH), rowb(SB_WIDTH), rowb(MEM_WIDTH),
                  full(GDN_WIDTH, d), full(SB_WIDTH, d), full(MEM_WIDTH, d),
                  rowb(d), full(1, d), full(1, d)],
        out_specs=[rowb(d), rowb(d)],
        out_shape=[jax.ShapeDtypeStruct((n, d), F32), jax.ShapeDtypeStruct((n, d), BF16)],
        compiler_params=_cparams(("parallel",), 40),
        name="out_proj_ln",
    )(o_gdn, o_sb, o_mem, wg, ws, wm, res, lw.reshape(1, d), lb.reshape(1, d))


def _peer_score_kernel(h_ref, wq_ref, k1_ref, k2_ref, s1_ref, s2_ref):
    q = _dot(h_ref[...], wq_ref[...]).astype(BF16)
    k1 = k1_ref[...]
    k2 = k2_ref[...]
    for h in range(PEER_HEADS):
        base = h * 2 * PEER_D_HALF
        s1_ref[h] = _dot_nt(k1, q[:, base:base + PEER_D_HALF])
        s2_ref[h] = _dot_nt(k2, q[:, base + PEER_D_HALF:base + 2 * PEER_D_HALF])


def _peer_scores(hb, wq, keys1, keys2):
    n, d = hb.shape
    tb = min(512, n)
    dq = wq.shape[1]
    out = jax.ShapeDtypeStruct((PEER_HEADS, PEER_N_KEYS, n), F32)
    ospec = pl.BlockSpec((PEER_HEADS, PEER_N_KEYS, tb), lambda i: (0, 0, i))
    return pl.pallas_call(
        _peer_score_kernel,
        grid=(n // tb,),
        in_specs=[pl.BlockSpec((tb, d), lambda i: (i, 0)),
                  pl.BlockSpec((d, dq), lambda i: (0, 0)),
                  pl.BlockSpec((PEER_N_KEYS, PEER_D_HALF), lambda i: (0, 0)),
                  pl.BlockSpec((PEER_N_KEYS, PEER_D_HALF), lambda i: (0, 0))],
        out_specs=[ospec, ospec],
        out_shape=[out, out],
        compiler_params=_cparams(("parallel",), 48),
        name="peer_scores",
    )(hb, wq, keys1, keys2)


NTOP = PEER_TOPK + 1
VROWS = 24


def _extract_top(cur, n, rowid, out_ref=None):
    vals = []
    nrows = cur.shape[0]
    for it in range(n):
        m = jnp.max(cur, axis=0, keepdims=True)
        first = jnp.min(jnp.where(cur == m, rowid, float(nrows)), axis=0, keepdims=True)
        cur = jnp.where(rowid == first, NEG_BIG, cur)
        vals.append(m)
        if out_ref is not None:
            out_ref[it:it + 1, :] = m
    return vals


def _extract_top_distinct(cur, n, out_ref):
    for it in range(n):
        m = jnp.max(cur, axis=0, keepdims=True)
        cur = jnp.where(cur == m, NEG_BIG, cur)
        out_ref[it:it + 1, :] = m
    return jnp.sum(jnp.where(cur == NEG_BIG, 1.0, 0.0), axis=0, keepdims=True)


def _peer_gate_kernel(s1_ref, s2_ref, t1_ref, d1_ref, e2_ref, v1_s, v2_s):
    s1 = s1_ref[0]
    s2 = s2_ref[0]
    tb = s1.shape[1]
    v1_s[...] = jnp.full(v1_s.shape, NEG_BIG, F32)
    v2_s[...] = jnp.full(v2_s.shape, NEG_BIG, F32)
    lost = jnp.maximum(_extract_top_distinct(s1, NTOP, v1_s), _extract_top_distinct(s2, NTOP, v2_s))

    @pl.when(jnp.max(lost) > NTOP)
    def _():
        rowid = lax.broadcasted_iota(jnp.int32, s1.shape, 0).astype(F32)
        _extract_top(s1, NTOP, rowid, v1_s)
        _extract_top(s2, NTOP, rowid, v2_s)

    v1 = [v1_s[a:a + 1, :] for a in range(8)]
    r8 = lax.broadcasted_iota(jnp.int32, (8, tb), 0)
    pieces = [v1[0] + v2_s[...]]
    for a in range(1, 8):
        nb = NTOP // (a + 1)
        pieces.append(jnp.where(r8 < nb, v1[a] + v2_s[0:8, :], NEG_BIG))
    pieces.append(v1_s[8:VROWS, :] + v2_s[0:1, :])
    cand = jnp.concatenate(pieces, axis=0)
    candid = lax.broadcasted_iota(jnp.int32, cand.shape, 0).astype(F32)
    top = _extract_top(cand, NTOP, candid)
    thr = 0.5 * (top[PEER_TOPK - 1] + top[PEER_TOPK])
    zsum = jnp.sum(jnp.where(cand >= thr, jnp.exp(cand - top[0]), 0.0), axis=0, keepdims=True)
    m2 = v2_s[0:1, :]
    t1_ref[0] = jnp.exp((thr - s1) - m2)
    d1_ref[0] = jnp.exp(s1 - v1[0]) / zsum
    e2_ref[0] = jnp.exp(s2 - m2)


def _peer_gates(s1t, s2t):
    nh, nk, n = s1t.shape
    tb = min(512, n)
    spec = pl.BlockSpec((1, nk, tb), lambda h, i: (h, 0, i))
    out = jax.ShapeDtypeStruct((nh, nk, n), F32)
    return pl.pallas_call(
        _peer_gate_kernel,
        grid=(nh, n // tb),
        in_specs=[spec, spec],
        out_specs=[spec, spec, spec],
        out_shape=[out, out, out],
        scratch_shapes=[pltpu.VMEM((VROWS, tb), F32), pltpu.VMEM((VROWS, tb), F32)],
        compiler_params=_cparams(("parallel", "parallel"), 32),
        name="peer_gates",
    )(s1t, s2t)


SQRT_HALF = 0.7071067811865476
PEER_GROUP = 8
PEER_LANES = 256


def _gelu(x):
    return 0.5 * x * (1.0 + lax.erf(x * SQRT_HALF))


def _peer_mix_kernel(x_ref, u_ref, vt_ref, t1_ref, d1_ref, e2_ref, o_ref, *scratch):
    nchunk = len(scratch) // 2
    act_refs = scratch[:nchunk]
    p_refs = scratch[nchunk:]
    g = pl.program_id(1)
    nk = PEER_N_KEYS

    def cols(c):
        return slice(c * PEER_LANES, (c + 1) * PEER_LANES)

    def scores(c):
        act_refs[c][...] = _dot_nt(u_ref[...], x_ref[cols(c), :])

    def weigh(c, i):
        cs = cols(c)
        rs = slice(i * nk, (i + 1) * nk)
        w = jnp.zeros((nk, PEER_LANES), F32)
        for h in range(PEER_HEADS):
            e2 = e2_ref[h, :, cs]
            w = w + jnp.where(e2 >= t1_ref[h, i:i + 1, cs], e2, 0.0) * d1_ref[h, i:i + 1, cs]
        p_refs[c][rs, :] = (_gelu(act_refs[c][rs, :]) * w).astype(BF16)

    def mix(c, pair, assign):
        rs = slice(pair * 2 * nk, (pair + 1) * 2 * nk)
        upd = _dot(vt_ref[:, rs], p_refs[c][rs, :])
        if assign:
            o_ref[:, cols(c)] = upd
        else:
            o_ref[:, cols(c)] += upd

    def step(first_group):
        scores(0)
        for c in range(nchunk):
            if c + 1 < nchunk:
                scores(c + 1)
            for pair in range(PEER_GROUP // 2):
                weigh(c, 2 * pair)
                weigh(c, 2 * pair + 1)
                if pair >= 1:
                    mix(c, pair - 1, first_group and pair == 1)
            mix(c, PEER_GROUP // 2 - 1, False)

    @pl.when(g == 0)
    def _():
        step(True)

    @pl.when(g != 0)
    def _():
        step(False)


def _peer_mix(hb, u_b, vt_b, t1t, d1t, e2t):
    n, d = hb.shape
    tb = min(1024, n)
    ne = u_b.shape[0]
    eb = PEER_GROUP * PEER_N_KEYS
    ng = ne // eb
    gspec = pl.BlockSpec((PEER_HEADS, PEER_GROUP, tb), lambda t, g: (0, g, t))
    kspec = pl.BlockSpec((PEER_HEADS, PEER_N_KEYS, tb), lambda t, g: (0, 0, t))
    nchunk = tb // PEER_LANES
    return pl.pallas_call(
        _peer_mix_kernel,
        grid=(n // tb, ng),
        in_specs=[pl.BlockSpec((tb, d), lambda t, g: (t, 0)),
                  pl.BlockSpec((eb, d), lambda t, g: (g, 0)),
                  pl.BlockSpec((d, eb), lambda t, g: (0, g)),
                  gspec, gspec, kspec],
        out_specs=pl.BlockSpec((d, tb), lambda t, g: (0, t)),
        out_shape=jax.ShapeDtypeStruct((d, n), F32),
        scratch_shapes=([pltpu.VMEM((eb, PEER_LANES), F32)] * nchunk
                        + [pltpu.VMEM((eb, PEER_LANES), BF16)] * nchunk),
        compiler_params=_cparams(("parallel", "arbitrary"), 60),
        name="peer_mix",
    )(hb, u_b, vt_b, t1t, d1t, e2t)


def _final_ln_kernel(h_ref, ft_ref, w_ref, b_ref, o_ref):
    y = DEEPNORM_ALPHA * h_ref[...] + ft_ref[...].T
    o_ref[...] = _layer_norm_rows(y, w_ref[...], b_ref[...])


def _final_ln(h, ffn_t, w, b):
    n, d = h.shape
    tm = min(512, n)
    return pl.pallas_call(
        _final_ln_kernel,
        grid=(n // tm,),
        in_specs=[pl.BlockSpec((tm, d), lambda i: (i, 0)),
                  pl.BlockSpec((d, tm), lambda i: (0, i)),
                  pl.BlockSpec((1, d), lambda i: (0, 0)),
                  pl.BlockSpec((1, d), lambda i: (0, 0))],
        out_specs=pl.BlockSpec((tm, d), lambda i: (i, 0)),
        out_shape=jax.ShapeDtypeStruct((n, d), F32),
        compiler_params=_cparams(("parallel",), 40),
        name="final_ln",
    )(h, ffn_t, w.reshape(1, d), b.reshape(1, d))


def _reorder_w_in(w_in):
    o_ab = 4 * GDN_WIDTH
    d = w_in.shape[0]
    main = jnp.concatenate([w_in[:, :o_ab], w_in[:, o_ab + 2 * GDN_HEADS:]], axis=1)
    ab = jnp.zeros((d, 128), w_in.dtype).at[:, :2 * GDN_HEADS].set(w_in[:, o_ab:o_ab + 2 * GDN_HEADS])
    return jnp.concatenate([main, ab], axis=1)


def kernel(x, mem, ln_emb_w, ln_emb_b, w_in, conv_w, a_log, dt_bias, gdn_norm_w, sb_norm_w,
           mem_norm_w, ln_mem_w, ln_mem_b, w_mem_kv, w_out, ln1_w, ln1_b, peer_wq, peer_keys1,
           peer_keys2, peer_u, peer_v, ln2_w, ln2_b):
    batch, seq, d = x.shape
    n_mem = mem.shape[1]
    l = 0
    h0, h0b = _ln(x.reshape(batch * seq, d), ln_emb_w, ln_emb_b)
    proj = _mm(h0b, _reorder_w_in(w_in[l]).astype(BF16), PROJ_COLS // 7, "in_proj")

    o_gdn = _gdn(proj, conv_w[l], a_log[l], dt_bias[l], gdn_norm_w[l], batch, seq)
    o_sb = _sb(proj, sb_norm_w[l], batch, seq)
    _, mb = _ln(mem.reshape(batch * n_mem, d), ln_mem_w[l], ln_mem_b[l])
    kv = _mm(mb, w_mem_kv[l].astype(BF16), 2 * MEM_WIDTH // 2, "mem_kv")
    o_mem = _mem_attn(proj, kv, mem_norm_w[l], batch, seq, n_mem)

    h1, h1b = _out_ln(o_gdn, o_sb, o_mem, w_out[l].astype(BF16), h0, ln1_w[l], ln1_b[l])

    s1t, s2t = _peer_scores(h1b, peer_wq[l].astype(BF16), peer_keys1[l].astype(BF16), peer_keys2[l].astype(BF16))
    t1t, d1t, e2t = _peer_gates(s1t, s2t)
    ffn_t = _peer_mix(h1b, peer_u[l].astype(BF16), peer_v[l].astype(BF16).T, t1t, d1t, e2t)
    out = _final_ln(h1, ffn_t, ln2_w[l], ln2_b[l])
    return out.reshape(batch, seq, d)
```

```python
import functools

import jax
import jax.numpy as jnp
from jax import lax
from jax.experimental import pallas as pl
from jax.experimental.pallas import tpu as pltpu

F32 = jnp.float32
BF16 = jnp.bfloat16

HEAD_DIM = 128
GDN_HEADS = 8
SB_HEADS = 4
MEM_HEADS = 4
GDN_WIDTH = GDN_HEADS * HEAD_DIM
SB_WIDTH = SB_HEADS * HEAD_DIM
MEM_WIDTH = MEM_HEADS * HEAD_DIM
CONV_WIDTH = 4
PEER_HEADS = 8
PEER_N_KEYS = 128
PEER_TOPK = 16
PEER_D_HALF = 128
DEPTH = 1
DEEPNORM_ALPHA = (2 * DEPTH) ** 0.25
LN_EPS = 1e-5
RMS_EPS = 1e-6
ATTN_SCALE = HEAD_DIM ** -0.5

COL_GQ = 0
COL_GK = GDN_WIDTH
COL_GV = 2 * GDN_WIDTH
COL_GZ = 3 * GDN_WIDTH
COL_SQ = 4 * GDN_WIDTH
COL_SK = COL_SQ + SB_WIDTH
COL_SV = COL_SK + SB_WIDTH
COL_MQ = COL_SV + SB_WIDTH
COL_AB = COL_MQ + MEM_WIDTH
MXU_WIDTH = 256
PROJ_TILE = 5 * MXU_WIDTH
PROJ_COLS = -(-(COL_AB + 128) // PROJ_TILE) * PROJ_TILE

GDN_CHUNK = 128
HALO = 8
NEG_BIG = -3.0e38
SB_LOG_CUTOFF = -104.0
MIB = 1024 * 1024


def _cparams(sem, vmem_mib):
    return pltpu.CompilerParams(dimension_semantics=sem, vmem_limit_bytes=vmem_mib * MIB)


def _dot(a, b):
    return jnp.dot(a, b, preferred_element_type=F32)


def _dot_nt(a, b):
    return lax.dot_general(a, b, (((1,), (1,)), ((), ())), preferred_element_type=F32)


def _split(a):
    hi = a.astype(BF16)
    lo = (a - hi.astype(F32)).astype(BF16)
    return hi, lo


def _dot3(a, b):
    ah, al = _split(a)
    bh, bl = _split(b)
    return _dot(ah, bh) + (_dot(ah, bl) + _dot(al, bh))


def _dot3_many(as_, bs_):
    sa = [_split(a) for a in as_]
    sb = [_split(b) for b in bs_]
    hh = [_dot(x[0], y[0]) for x, y in zip(sa, sb)]
    hl = [_dot(x[0], y[1]) for x, y in zip(sa, sb)]
    lh = [_dot(x[1], y[0]) for x, y in zip(sa, sb)]
    return [p + (m + n) for p, m, n in zip(hh, hl, lh)]


def _softplus(x):
    return jnp.maximum(x, 0.0) + jnp.log(1.0 + jnp.exp(-jnp.abs(x)))


def _sigmoid(x):
    return 1.0 / (1.0 + jnp.exp(-x))


def _silu(x):
    return x * _sigmoid(x)


def _layer_norm_rows(x, w, b):
    mu = jnp.mean(x, axis=-1, keepdims=True)
    xc = x - mu
    var = jnp.mean(xc * xc, axis=-1, keepdims=True)
    return xc * lax.rsqrt(var + LN_EPS) * w + b


def _rms_rows(x, w):
    return x * lax.rsqrt(jnp.mean(x * x, axis=-1, keepdims=True) + RMS_EPS) * w


def _ln_kernel(x_ref, w_ref, b_ref, o_ref, ob_ref):
    y = _layer_norm_rows(x_ref[...], w_ref[...], b_ref[...])
    o_ref[...] = y
    ob_ref[...] = y.astype(BF16)


def _ln(x2d, w, b):
    n, d = x2d.shape
    tm = min(512, n)
    return pl.pallas_call(
        _ln_kernel,
        grid=(n // tm,),
        in_specs=[pl.BlockSpec((tm, d), lambda i: (i, 0)),
                  pl.BlockSpec((1, d), lambda i: (0, 0)),
                  pl.BlockSpec((1, d), lambda i: (0, 0))],
        out_specs=[pl.BlockSpec((tm, d), lambda i: (i, 0)),
                   pl.BlockSpec((tm, d), lambda i: (i, 0))],
        out_shape=[jax.ShapeDtypeStruct((n, d), F32), jax.ShapeDtypeStruct((n, d), BF16)],
        compiler_params=_cparams(("parallel",), 32),
        name="layer_norm",
    )(x2d, w.reshape(1, d), b.reshape(1, d))


def _mm_kernel(a_ref, w_ref, o_ref):
    o_ref[...] = _dot(a_ref[...], w_ref[...]).astype(o_ref.dtype)


def _mm(a, w, tn, name):
    m, k = a.shape
    n = w.shape[1]
    tm = min(1024, m)
    return pl.pallas_call(
        _mm_kernel,
        grid=(m // tm, n // tn),
        in_specs=[pl.BlockSpec((tm, k), lambda i, j: (i, 0)),
                  pl.BlockSpec((k, tn), lambda i, j: (0, j))],
        out_specs=pl.BlockSpec((tm, tn), lambda i, j: (i, j)),
        out_shape=jax.ShapeDtypeStruct((m, n), F32),
        compiler_params=_cparams(("parallel", "arbitrary"), 40),
        name=name,
    )(a, w)


def _ln_mm_kernel(x_ref, lw_ref, lb_ref, w_ref, o_ref, hb_s):
    @pl.when(pl.program_id(1) == 0)
    def _():
        hb_s[...] = _layer_norm_rows(x_ref[...], lw_ref[...], lb_ref[...]).astype(BF16)

    o_ref[...] = _dot(hb_s[...], w_ref[...])


def _ln_mm(x2d, lw, lb, w, tn):
    m, k = x2d.shape
    n = w.shape[1]
    tm = min(1024, m)
    return pl.pallas_call(
        _ln_mm_kernel,
        grid=(m // tm, n // tn),
        in_specs=[pl.BlockSpec((tm, k), lambda i, j: (i, 0)),
                  pl.BlockSpec((1, k), lambda i, j: (0, 0)),
                  pl.BlockSpec((1, k), lambda i, j: (0, 0)),
                  pl.BlockSpec((k, tn), lambda i, j: (0, j))],
        out_specs=pl.BlockSpec((tm, tn), lambda i, j: (i, j)),
        out_shape=jax.ShapeDtypeStruct((m, n), F32),
        scratch_shapes=[pltpu.VMEM((tm, k), BF16)],
        compiler_params=_cparams(("parallel", "arbitrary"), 56),
        name="ln_in_proj",
    )(x2d, lw.reshape(1, k), lb.reshape(1, k), w)


def _gdn_kernel(q_ref, k_ref, v_ref, z_ref, ab_ref, cwq_ref, cwk_ref, cwv_ref, alog_ref, dtb_ref, nw_ref,
                o_ref, xbuf, s_ref, *, hps, rows):
    C = GDN_CHUNK
    hg = pl.program_id(1)
    tt = pl.program_id(2)

    @pl.when(tt == 0)
    def _():
        xbuf[:, 0:HALO, :] = jnp.zeros((3, HALO, hps * HEAD_DIM), F32)
        s_ref[...] = jnp.zeros_like(s_ref)

    xbuf[0, HALO:HALO + rows, :] = q_ref[...]
    xbuf[1, HALO:HALO + rows, :] = k_ref[...]
    xbuf[2, HALO:HALO + rows, :] = v_ref[...]

    row = lax.broadcasted_iota(jnp.int32, (C, C), 0)
    col = lax.broadcasted_iota(jnp.int32, (C, C), 1)
    tril = row >= col
    strict = row > col
    eye = (row == col).astype(F32)
    tril_f = tril.astype(F32)
    diag16 = strict & ((row // 16) == (col // 16))
    levels = [strict & ((row // (2 * s)) == (col // (2 * s))) & ((row // s) != (col // s)) for s in (16, 32, 64)]
    lane = lax.broadcasted_iota(jnp.int32, (C, HEAD_DIM), 1)
    neg_a = -jnp.exp(alog_ref[...])
    dtb = dtb_ref[...]
    nw = nw_ref[...]
    cws = (cwq_ref[...], cwk_ref[...], cwv_ref[...])

    def conv(which, r0, sl):
        win = xbuf[which, pl.ds(r0, C + HALO), sl]
        cw = cws[which][:, sl]
        y = win[HALO:] * cw[CONV_WIDTH - 1:CONV_WIDTH, :]
        for back in range(1, CONV_WIDTH):
            shifted = pltpu.roll(win, back, 0)[HALO:]
            y = y + shifted * cw[CONV_WIDTH - 1 - back:CONV_WIDTH - back, :]
        return _silu(y)

    def chunk(c, carry):
        r0 = pl.multiple_of(c * C, C)
        ab = ab_ref[pl.ds(r0, C), :]
        g_all = neg_a * _softplus(ab + dtb)
        beta_all = _sigmoid(ab)
        gcum_all = _dot3(tril_f, g_all)
        gcum_t = gcum_all.T
        hs = range(hps)
        sls = [slice(j * HEAD_DIM, (j + 1) * HEAD_DIM) for j in hs]
        heads = [hg * hps + j for j in hs]
        q = [conv(0, r0, sl) for sl in sls]
        k = [conv(1, r0, sl) for sl in sls]
        v = [conv(2, r0, sl) for sl in sls]
        q = [x * lax.rsqrt(jnp.sum(x * x, axis=-1, keepdims=True) + RMS_EPS) * ATTN_SCALE for x in q]
        k = [x * lax.rsqrt(jnp.sum(x * x, axis=-1, keepdims=True) + RMS_EPS) for x in k]
        gcol = [jnp.sum(jnp.where(lane == h, gcum_all, 0.0), axis=1, keepdims=True) for h in heads]
        grow = [jnp.sum(jnp.where(row == h, gcum_t, 0.0), axis=0, keepdims=True) for h in heads]
        beta = [jnp.sum(jnp.where(lane == GDN_HEADS + h, beta_all, 0.0), axis=1, keepdims=True) for h in heads]
        decay = [jnp.where(tril, jnp.exp(jnp.where(tril, gcol[j] - grow[j], 0.0)), 0.0) for j in hs]
        kb = [k[j] * beta[j] for j in hs]
        vb = [v[j] * beta[j] for j in hs]
        kk = _dot3_many(kb, [x.T for x in k])
        low = [jnp.where(strict, kk[j] * decay[j], 0.0) for j in hs]
        a = [jnp.where(diag16, -low[j], 0.0) for j in hs]
        inv = [eye + a[j] for j in hs]
        for _ in range(3):
            a = _dot3_many(a, a)
            t = _dot3_many(inv, a)
            inv = [inv[j] + t[j] for j in hs]
        for lv in levels:
            t = _dot3_many([jnp.where(lv, low[j], 0.0) for j in hs], inv)
            t = _dot3_many(inv, t)
            inv = [inv[j] - t[j] for j in hs]
        eg = [jnp.exp(gcol[j]) for j in hs]
        uw = _dot3_many(inv, [jnp.concatenate([vb[j], kb[j] * eg[j]], axis=1) for j in hs])
        qk = [_dot_nt(q[j].astype(BF16), k[j].astype(BF16)) for j in hs]
        qk = [jnp.where(tril, qk[j] * decay[j], 0.0).astype(BF16) for j in hs]
        s = [s_ref[j] for j in hs]
        sb = [x.astype(BF16) for x in s]
        ws = [_dot(uw[j][:, HEAD_DIM:].astype(BF16), sb[j]) for j in hs]
        o1 = [_dot((q[j] * eg[j]).astype(BF16), sb[j]) for j in hs]
        vnb = [(uw[j][:, :HEAD_DIM] - ws[j]).astype(BF16) for j in hs]
        o2 = [_dot(qk[j], vnb[j]) for j in hs]
        g_last = [gcol[j][C - 1:C, :] for j in hs]
        kdt = [(k[j] * jnp.exp(g_last[j] - gcol[j])).T.astype(BF16) for j in hs]
        upd = [_dot(kdt[j], vnb[j]) for j in hs]
        for j in hs:
            s_ref[j] = s[j] * jnp.exp(g_last[j]) + upd[j]
            z = z_ref[pl.ds(r0, C), sls[j]]
            o_ref[pl.ds(r0, C), sls[j]] = (_rms_rows(o1[j] + o2[j], nw) * _silu(z)).astype(o_ref.dtype)
        return carry

    lax.fori_loop(0, rows // C, chunk, 0)
    xbuf[:, 0:HALO, :] = xbuf[:, rows:rows + HALO, :]


def _gdn(proj, conv_w, a_log, dt_bias, gdn_norm_w, batch, seq):
    hps = 8
    rows = min(512, seq)
    nt = seq // rows
    w = hps * HEAD_DIM
    cwq, cwk, cwv = (conv_w[:, i * GDN_WIDTH:(i + 1) * GDN_WIDTH] for i in range(3))
    pad = lambda t: jnp.zeros((1, 128), F32).at[0, :GDN_HEADS].set(t)

    def col(base):
        return lambda b, g, t: (b * nt + t, base // w + g)

    return pl.pallas_call(
        functools.partial(_gdn_kernel, hps=hps, rows=rows),
        grid=(batch, GDN_HEADS // hps, nt),
        in_specs=[pl.BlockSpec((rows, w), col(COL_GQ)),
                  pl.BlockSpec((rows, w), col(COL_GK)),
                  pl.BlockSpec((rows, w), col(COL_GV)),
                  pl.BlockSpec((rows, w), col(COL_GZ)),
                  pl.BlockSpec((rows, 128), lambda b, g, t: (b * nt + t, COL_AB // 128)),
                  pl.BlockSpec((CONV_WIDTH, w), lambda b, g, t: (0, g)),
                  pl.BlockSpec((CONV_WIDTH, w), lambda b, g, t: (0, g)),
                  pl.BlockSpec((CONV_WIDTH, w), lambda b, g, t: (0, g)),
                  pl.BlockSpec((1, 128), lambda b, g, t: (0, 0)),
                  pl.BlockSpec((1, 128), lambda b, g, t: (0, 0)),
                  pl.BlockSpec((1, 128), lambda b, g, t: (0, 0))],
        out_specs=pl.BlockSpec((rows, w), lambda b, g, t: (b * nt + t, g)),
        out_shape=jax.ShapeDtypeStruct((batch * seq, GDN_WIDTH), BF16),
        scratch_shapes=[pltpu.VMEM((3, rows + HALO, w), F32), pltpu.VMEM((hps, HEAD_DIM, HEAD_DIM), F32)],
        compiler_params=_cparams(("parallel", "parallel", "arbitrary"), 32),
        name="gated_deltanet",
    )(proj, proj, proj, proj, proj, cwq, cwk, cwv, pad(a_log), pad(dt_bias), gdn_norm_w.reshape(1, 128))


def _sb_kernel(q_ref, k_ref, v_ref, nw_ref, o_ref, *, tq, tk, hps):
    qi = pl.program_id(2)
    q0 = qi * tq
    hs = range(hps)
    sls = [slice(j * HEAD_DIM, (j + 1) * HEAD_DIM) for j in hs]
    q = [q_ref[:, sl].astype(BF16) for sl in sls]
    rowi = lax.broadcasted_iota(jnp.int32, (tq, tk), 0)
    coli = lax.broadcasted_iota(jnp.int32, (tq, tk), 1)
    jj = lax.broadcasted_iota(jnp.int32, (tk, tk), 0)
    ss = lax.broadcasted_iota(jnp.int32, (tk, tk), 1)
    later = (jj > ss).astype(BF16)
    nk = (qi + 1) * (tq // tk)

    def cond(carry):
        i, _, _, live = carry
        return jnp.logical_and(i < nk, live > SB_LOG_CUTOFF)

    def body(carry):
        i, acc, run, _ = carry
        k0 = pl.multiple_of((nk - 1 - i) * tk, tk)
        causal = (coli + k0) < (rowi + q0)
        kb = [k_ref[pl.ds(k0, tk), sl].astype(BF16) for sl in sls]
        vb = [v_ref[pl.ds(k0, tk), sl].astype(BF16) for sl in sls]
        z = [_dot_nt(q[j], kb[j]) * ATTN_SCALE for j in hs]
        sp = [_softplus(z[j]) for j in hs]
        log_not = [jnp.where(causal, -sp[j], 0.0) for j in hs]
        parts = [_split(log_not[j]) for j in hs]
        after = [_dot(parts[j][0], later) + _dot(parts[j][1], later) + run[j] for j in hs]
        a = [jnp.where(causal, jnp.exp((z[j] - sp[j]) + after[j]), 0.0).astype(BF16) for j in hs]
        acc = tuple(acc[j] + _dot(a[j], vb[j]) for j in hs)
        run = tuple(run[j] + jnp.sum(log_not[j], axis=1, keepdims=True) for j in hs)
        live = jnp.max(run[0])
        for j in range(1, hps):
            live = jnp.maximum(live, jnp.max(run[j]))
        return i + 1, acc, run, live

    init = (jnp.int32(0), tuple(jnp.zeros((tq, HEAD_DIM), F32) for _ in hs),
            tuple(jnp.zeros((tq, 1), F32) for _ in hs), jnp.float32(0.0))
    _, acc, _, _ = lax.while_loop(cond, body, init)
    for j in hs:
        o_ref[:, sls[j]] = _rms_rows(acc[j], nw_ref[:, sls[j]]).astype(o_ref.dtype)


def _sb(proj, sb_norm_w, batch, seq):
    tq = min(256, seq)
    tk = 128
    hps = 4
    w = hps * HEAD_DIM
    nq = seq // tq
    cq, ck, cv = COL_SQ // w, COL_SK // w, COL_SV // w
    return pl.pallas_call(
        functools.partial(_sb_kernel, tq=tq, tk=tk, hps=hps),
        grid=(batch, SB_HEADS // hps, nq),
        in_specs=[pl.BlockSpec((tq, w), lambda b, h, i: (b * nq + i, cq + h)),
                  pl.BlockSpec((seq, w), lambda b, h, i: (b, ck + h)),
                  pl.BlockSpec((seq, w), lambda b, h, i: (b, cv + h)),
                  pl.BlockSpec((1, w), lambda b, h, i: (0, h))],
        out_specs=pl.BlockSpec((tq, w), lambda b, h, i: (b * nq + i, h)),
        out_shape=jax.ShapeDtypeStruct((batch * seq, SB_WIDTH), BF16),
        compiler_params=_cparams(("parallel", "parallel", "arbitrary"), 48),
        name="stick_breaking",
    )(proj, proj, proj, sb_norm_w.reshape(1, SB_WIDTH))


def _mem_kernel(q_ref, kv_ref, nw_ref, o_ref):
    for h in range(MEM_HEADS):
        sl = slice(h * HEAD_DIM, (h + 1) * HEAD_DIM)
        q = q_ref[:, sl].astype(BF16)
        k = kv_ref[:, sl].astype(BF16)
        v = kv_ref[:, MEM_WIDTH + h * HEAD_DIM:MEM_WIDTH + (h + 1) * HEAD_DIM].astype(BF16)
        s = _dot_nt(q, k) * ATTN_SCALE
        e = jnp.exp(s - jnp.max(s, axis=-1, keepdims=True))
        p = e / jnp.sum(e, axis=-1, keepdims=True)
        o = _dot(p.astype(BF16), v)
        o_ref[:, sl] = _rms_rows(o, nw_ref[:, sl]).astype(o_ref.dtype)


def _mem_attn(proj, kv, mem_norm_w, batch, seq, n_mem):
    tq = min(512, seq)
    nq = seq // tq
    return pl.pallas_call(
        _mem_kernel,
        grid=(batch, nq),
        in_specs=[pl.BlockSpec((tq, MEM_WIDTH), lambda b, i: (b * nq + i, COL_MQ // MEM_WIDTH)),
                  pl.BlockSpec((n_mem, 2 * MEM_WIDTH), lambda b, i: (b, 0)),
                  pl.BlockSpec((1, MEM_WIDTH), lambda b, i: (0, 0))],
        out_specs=pl.BlockSpec((tq, MEM_WIDTH), lambda b, i: (b * nq + i, 0)),
        out_shape=jax.ShapeDtypeStruct((batch * seq, MEM_WIDTH), BF16),
        compiler_params=_cparams(("parallel", "parallel"), 32),
        name="memory_attention",
    )(proj, kv, mem_norm_w.reshape(1, MEM_WIDTH))


def _out_ln_kernel(og_ref, os_ref, om_ref, wg_ref, ws_ref, wm_ref, x_ref, ew_ref, eb_ref, lw_ref, lb_ref,
                   o_ref, ob_ref):
    mix = _dot(og_ref[...], wg_ref[...]) + _dot(os_ref[...], ws_ref[...]) + _dot(om_ref[...], wm_ref[...])
    res = _layer_norm_rows(x_ref[...], ew_ref[...], eb_ref[...])
    y = _layer_norm_rows(DEEPNORM_ALPHA * res + mix, lw_ref[...], lb_ref[...])
    o_ref[...] = y
    ob_ref[...] = y.astype(BF16)


def _out_ln(o_gdn, o_sb, o_mem, w_out, x2d, ew, eb, lw, lb):
    n, d = x2d.shape
    tm = min(256, n)
    wg, ws, wm = w_out[:GDN_WIDTH], w_out[GDN_WIDTH:GDN_WIDTH + SB_WIDTH], w_out[GDN_WIDTH + SB_WIDTH:]
    rowb = lambda width: pl.BlockSpec((tm, width), lambda i: (i, 0))
    full = lambda r, c: pl.BlockSpec((r, c), lambda i: (0, 0))
    return pl.pallas_call(
        _out_ln_kernel,
        grid=(n // tm,),
        in_specs=[rowb(GDN_WIDTH), rowb(SB_WIDTH), rowb(MEM_WIDTH),
                  full(GDN_WIDTH, d), full(SB_WIDTH, d), full(MEM_WIDTH, d),
                  rowb(d), full(1, d), full(1, d), full(1, d), full(1, d)],
        out_specs=[rowb(d), rowb(d)],
        out_shape=[jax.ShapeDtypeStruct((n, d), F32), jax.ShapeDtypeStruct((n, d), BF16)],
        compiler_params=_cparams(("parallel",), 40),
        name="out_proj_ln",
    )(o_gdn, o_sb, o_mem, wg, ws, wm, x2d, ew.reshape(1, d), eb.reshape(1, d), lw.reshape(1, d), lb.reshape(1, d))


def _peer_score_kernel(h_ref, wq_ref, k1_ref, k2_ref, s1_ref, s2_ref):
    q = _dot(h_ref[...], wq_ref[...]).astype(BF16)
    k1 = k1_ref[...]
    k2 = k2_ref[...]
    for h in range(PEER_HEADS):
        base = h * 2 * PEER_D_HALF
        s1_ref[h] = _dot_nt(k1, q[:, base:base + PEER_D_HALF])
        s2_ref[h] = _dot_nt(k2, q[:, base + PEER_D_HALF:base + 2 * PEER_D_HALF])


def _peer_scores(hb, wq, keys1, keys2):
    n, d = hb.shape
    tb = min(512, n)
    dq = wq.shape[1]
    out = jax.ShapeDtypeStruct((PEER_HEADS, PEER_N_KEYS, n), F32)
    ospec = pl.BlockSpec((PEER_HEADS, PEER_N_KEYS, tb), lambda i: (0, 0, i))
    return pl.pallas_call(
        _peer_score_kernel,
        grid=(n // tb,),
        in_specs=[pl.BlockSpec((tb, d), lambda i: (i, 0)),
                  pl.BlockSpec((d, dq), lambda i: (0, 0)),
                  pl.BlockSpec((PEER_N_KEYS, PEER_D_HALF), lambda i: (0, 0)),
                  pl.BlockSpec((PEER_N_KEYS, PEER_D_HALF), lambda i: (0, 0))],
        out_specs=[ospec, ospec],
        out_shape=[out, out],
        compiler_params=_cparams(("parallel",), 48),
        name="peer_scores",
    )(hb, wq, keys1, keys2)


NTOP = PEER_TOPK + 1
VROWS = 24


def _extract_top(cur, n, rowid, out_ref=None):
    vals = []
    nrows = cur.shape[0]
    for it in range(n):
        m = jnp.max(cur, axis=0, keepdims=True)
        first = jnp.min(jnp.where(cur == m, rowid, float(nrows)), axis=0, keepdims=True)
        cur = jnp.where(rowid == first, NEG_BIG, cur)
        vals.append(m)
        if out_ref is not None:
            out_ref[it:it + 1, :] = m
    return vals


def _extract_top_distinct(cur, n, out_ref):
    for it in range(n):
        m = jnp.max(cur, axis=0, keepdims=True)
        cur = jnp.where(cur == m, NEG_BIG, cur)
        out_ref[it:it + 1, :] = m
    return jnp.sum(jnp.where(cur == NEG_BIG, 1.0, 0.0), axis=0, keepdims=True)


def _peer_gate_kernel(s1_ref, s2_ref, t1_ref, d1_ref, e2_ref, v1_s, v2_s, top_s):
    s1 = s1_ref[0]
    s2 = s2_ref[0]
    tb = s1.shape[1]
    v1_s[...] = jnp.full(v1_s.shape, NEG_BIG, F32)
    v2_s[...] = jnp.full(v2_s.shape, NEG_BIG, F32)
    lost = jnp.maximum(_extract_top_distinct(s1, NTOP, v1_s), _extract_top_distinct(s2, NTOP, v2_s))

    @pl.when(jnp.max(lost) > NTOP)
    def _():
        rowid = lax.broadcasted_iota(jnp.int32, s1.shape, 0).astype(F32)
        _extract_top(s1, NTOP, rowid, v1_s)
        _extract_top(s2, NTOP, rowid, v2_s)

    v1 = [v1_s[a:a + 1, :] for a in range(8)]
    r8 = lax.broadcasted_iota(jnp.int32, (8, tb), 0)
    pieces = [v1[0] + v2_s[...]]
    pads = 2 * (VROWS - NTOP)
    for a in range(1, 8):
        nb = NTOP // (a + 1)
        pads += 8 - nb
        pieces.append(jnp.where(r8 < nb, v1[a] + v2_s[0:8, :], NEG_BIG))
    pieces.append(v1_s[8:VROWS, :] + v2_s[0:1, :])
    cand = jnp.concatenate(pieces, axis=0)
    lost = _extract_top_distinct(cand, NTOP, top_s)

    @pl.when(jnp.max(lost) > NTOP + pads)
    def _():
        candid = lax.broadcasted_iota(jnp.int32, cand.shape, 0).astype(F32)
        _extract_top(cand, NTOP, candid, top_s)

    top0 = top_s[0:1, :]
    thr = 0.5 * (top_s[PEER_TOPK - 1:PEER_TOPK, :] + top_s[PEER_TOPK:PEER_TOPK + 1, :])
    zsum = jnp.sum(jnp.where(cand >= thr, jnp.exp(cand - top0), 0.0), axis=0, keepdims=True)
    m2 = v2_s[0:1, :]
    t1_ref[0] = jnp.exp((thr - s1) - m2)
    d1_ref[0] = jnp.exp(s1 - v1[0]) / zsum
    e2_ref[0] = jnp.exp(s2 - m2)


def _peer_gates(s1t, s2t):
    nh, nk, n = s1t.shape
    tb = min(512, n)
    spec = pl.BlockSpec((1, nk, tb), lambda h, i: (h, 0, i))
    out = jax.ShapeDtypeStruct((nh, nk, n), F32)
    return pl.pallas_call(
        _peer_gate_kernel,
        grid=(nh, n // tb),
        in_specs=[spec, spec],
        out_specs=[spec, spec, spec],
        out_shape=[out, out, out],
        scratch_shapes=[pltpu.VMEM((VROWS, tb), F32)] * 3,
        compiler_params=_cparams(("parallel", "parallel"), 32),
        name="peer_gates",
    )(s1t, s2t)


SQRT_HALF = 0.7071067811865476
PEER_GROUP = 8
PEER_LANES = 256


def _gelu(x):
    return 0.5 * x * (1.0 + lax.erf(x * SQRT_HALF))


def _peer_mix_kernel(x_ref, u_ref, vt_ref, t1_ref, d1_ref, e2_ref, o_ref, *scratch):
    nchunk = len(scratch) // 2
    act_refs = scratch[:nchunk]
    p_refs = scratch[nchunk:]
    g = pl.program_id(1)
    nk = PEER_N_KEYS

    def cols(c):
        return slice(c * PEER_LANES, (c + 1) * PEER_LANES)

    def scores(c):
        act_refs[c][...] = _dot_nt(u_ref[...], x_ref[cols(c), :])

    def weigh(c, i):
        cs = cols(c)
        rs = slice(i * nk, (i + 1) * nk)
        w = jnp.zeros((nk, PEER_LANES), F32)
        for h in range(PEER_HEADS):
            e2 = e2_ref[h, :, cs]
            w = w + jnp.where(e2 >= t1_ref[h, i:i + 1, cs], e2, 0.0) * d1_ref[h, i:i + 1, cs]
        p_refs[c][rs, :] = (_gelu(act_refs[c][rs, :]) * w).astype(BF16)

    def mix(c, pair, assign):
        rs = slice(pair * 2 * nk, (pair + 1) * 2 * nk)
        upd = _dot(vt_ref[:, rs], p_refs[c][rs, :])
        if assign:
            o_ref[:, cols(c)] = upd
        else:
            o_ref[:, cols(c)] += upd

    def step(first_group):
        scores(0)
        for c in range(nchunk):
            if c + 1 < nchunk:
                scores(c + 1)
            for pair in range(PEER_GROUP // 2):
                weigh(c, 2 * pair)
                weigh(c, 2 * pair + 1)
                if pair >= 1:
                    mix(c, pair - 1, first_group and pair == 1)
            mix(c, PEER_GROUP // 2 - 1, False)

    @pl.when(g == 0)
    def _():
        step(True)

    @pl.when(g != 0)
    def _():
        step(False)


def _peer_mix(hb, u_b, vt_b, t1t, d1t, e2t):
    n, d = hb.shape
    tb = min(1024, n)
    ne = u_b.shape[0]
    eb = PEER_GROUP * PEER_N_KEYS
    ng = ne // eb
    gspec = pl.BlockSpec((PEER_HEADS, PEER_GROUP, tb), lambda t, g: (0, g, t))
    kspec = pl.BlockSpec((PEER_HEADS, PEER_N_KEYS, tb), lambda t, g: (0, 0, t))
    nchunk = tb // PEER_LANES
    return pl.pallas_call(
        _peer_mix_kernel,
        grid=(n // tb, ng),
        in_specs=[pl.BlockSpec((tb, d), lambda t, g: (t, 0)),
                  pl.BlockSpec((eb, d), lambda t, g: (g, 0)),
                  pl.BlockSpec((d, eb), lambda t, g: (0, g)),
                  gspec, gspec, kspec],
        out_specs=pl.BlockSpec((d, tb), lambda t, g: (0, t)),
        out_shape=jax.ShapeDtypeStruct((d, n), F32),
        scratch_shapes=([pltpu.VMEM((eb, PEER_LANES), F32)] * nchunk
                        + [pltpu.VMEM((eb, PEER_LANES), BF16)] * nchunk),
        compiler_params=_cparams(("parallel", "arbitrary"), 60),
        name="peer_mix",
    )(hb, u_b, vt_b, t1t, d1t, e2t)


def _final_ln_kernel(h_ref, ft_ref, w_ref, b_ref, o_ref):
    y = DEEPNORM_ALPHA * h_ref[...] + ft_ref[...].T
    o_ref[...] = _layer_norm_rows(y, w_ref[...], b_ref[...])


def _final_ln(h, ffn_t, w, b):
    n, d = h.shape
    tm = min(512, n)
    return pl.pallas_call(
        _final_ln_kernel,
        grid=(n // tm,),
        in_specs=[pl.BlockSpec((tm, d), lambda i: (i, 0)),
                  pl.BlockSpec((d, tm), lambda i: (0, i)),
                  pl.BlockSpec((1, d), lambda i: (0, 0)),
                  pl.BlockSpec((1, d), lambda i: (0, 0))],
        out_specs=pl.BlockSpec((tm, d), lambda i: (i, 0)),
        out_shape=jax.ShapeDtypeStruct((n, d), F32),
        compiler_params=_cparams(("parallel",), 40),
        name="final_ln",
    )(h, ffn_t, w.reshape(1, d), b.reshape(1, d))


def _reorder_w_in(w_in):
    o_ab = 4 * GDN_WIDTH
    d = w_in.shape[0]
    main = jnp.concatenate([w_in[:, :o_ab], w_in[:, o_ab + 2 * GDN_HEADS:]], axis=1)
    ab = jnp.zeros((d, PROJ_COLS - COL_AB), w_in.dtype).at[:, :2 * GDN_HEADS].set(w_in[:, o_ab:o_ab + 2 * GDN_HEADS])
    return jnp.concatenate([main, ab], axis=1)


def kernel(x, mem, ln_emb_w, ln_emb_b, w_in, conv_w, a_log, dt_bias, gdn_norm_w, sb_norm_w,
           mem_norm_w, ln_mem_w, ln_mem_b, w_mem_kv, w_out, ln1_w, ln1_b, peer_wq, peer_keys1,
           peer_keys2, peer_u, peer_v, ln2_w, ln2_b):
    batch, seq, d = x.shape
    n_mem = mem.shape[1]
    l = 0
    x2d = x.reshape(batch * seq, d)
    proj = _ln_mm(x2d, ln_emb_w, ln_emb_b, _reorder_w_in(w_in[l]).astype(BF16), PROJ_TILE)

    o_gdn = _gdn(proj, conv_w[l], a_log[l], dt_bias[l], gdn_norm_w[l], batch, seq)
    o_sb = _sb(proj, sb_norm_w[l], batch, seq)
    _, mb = _ln(mem.reshape(batch * n_mem, d), ln_mem_w[l], ln_mem_b[l])
    kv = _mm(mb, w_mem_kv[l].astype(BF16), 2 * MEM_WIDTH // 2, "mem_kv")
    o_mem = _mem_attn(proj, kv, mem_norm_w[l], batch, seq, n_mem)

    h1, h1b = _out_ln(o_gdn, o_sb, o_mem, w_out[l].astype(BF16), x2d, ln_emb_w, ln_emb_b, ln1_w[l], ln1_b[l])

    s1t, s2t = _peer_scores(h1b, peer_wq[l].astype(BF16), peer_keys1[l].astype(BF16), peer_keys2[l].astype(BF16))
    t1t, d1t, e2t = _peer_gates(s1t, s2t)
    ffn_t = _peer_mix(h1b, peer_u[l].astype(BF16), peer_v[l].astype(BF16).T, t1t, d1t, e2t)
    out = _final_ln(h1, ffn_t, ln2_w[l], ln2_b[l])
    return out.reshape(batch, seq, d)
```

```python
import functools

import jax
import jax.numpy as jnp
from jax import lax
from jax.experimental import pallas as pl
from jax.experimental.pallas import tpu as pltpu

F32 = jnp.float32
BF16 = jnp.bfloat16

HEAD_DIM = 128
GDN_HEADS = 8
SB_HEADS = 4
MEM_HEADS = 4
GDN_WIDTH = GDN_HEADS * HEAD_DIM
SB_WIDTH = SB_HEADS * HEAD_DIM
MEM_WIDTH = MEM_HEADS * HEAD_DIM
CONV_WIDTH = 4
PEER_HEADS = 8
PEER_N_KEYS = 128
PEER_TOPK = 16
PEER_D_HALF = 128
DEPTH = 1
DEEPNORM_ALPHA = (2 * DEPTH) ** 0.25
LN_EPS = 1e-5
RMS_EPS = 1e-6
ATTN_SCALE = HEAD_DIM ** -0.5

COL_GQ = 0
COL_GK = GDN_WIDTH
COL_GV = 2 * GDN_WIDTH
COL_GZ = 3 * GDN_WIDTH
COL_SQ = 4 * GDN_WIDTH
COL_SK = COL_SQ + SB_WIDTH
COL_SV = COL_SK + SB_WIDTH
COL_MQ = COL_SV + SB_WIDTH
COL_AB = COL_MQ + MEM_WIDTH
MXU_WIDTH = 256
PROJ_TILE = 5 * MXU_WIDTH
PROJ_COLS = -(-(COL_AB + 128) // PROJ_TILE) * PROJ_TILE

GDN_CHUNK = 128
HALO = 8
NEG_BIG = -3.0e38
SB_LOG_CUTOFF = -104.0
MIB = 1024 * 1024


def _cparams(sem, vmem_mib):
    return pltpu.CompilerParams(dimension_semantics=sem, vmem_limit_bytes=vmem_mib * MIB)


def _dot(a, b):
    return jnp.dot(a, b, preferred_element_type=F32)


def _dot_nt(a, b):
    return lax.dot_general(a, b, (((1,), (1,)), ((), ())), preferred_element_type=F32)


def _split(a):
    hi = a.astype(BF16)
    lo = (a - hi.astype(F32)).astype(BF16)
    return hi, lo


def _dot3(a, b):
    return _dot3_many([a], [b])[0]


def _dot3_many(as_, bs_):
    sa = [_split(a) for a in as_]
    sb = [_split(b) for b in bs_]
    both = [_dot(jnp.concatenate(x, axis=0), y[0]) for x, y in zip(sa, sb)]
    hl = [_dot(x[0], y[1]) for x, y in zip(sa, sb)]
    return [t[:a.shape[0]] + (t[a.shape[0]:] + m) for a, t, m in zip(as_, both, hl)]


def _softplus(x):
    return jnp.maximum(x, 0.0) + jnp.log(1.0 + jnp.exp(-jnp.abs(x)))


def _sigmoid(x):
    return 1.0 / (1.0 + jnp.exp(-x))


def _silu(x):
    return x * _sigmoid(x)


def _layer_norm_rows(x, w, b):
    mu = jnp.mean(x, axis=-1, keepdims=True)
    xc = x - mu
    var = jnp.mean(xc * xc, axis=-1, keepdims=True)
    return xc * lax.rsqrt(var + LN_EPS) * w + b


def _rms_rows(x, w):
    return x * lax.rsqrt(jnp.mean(x * x, axis=-1, keepdims=True) + RMS_EPS) * w


def _ln_kernel(x_ref, w_ref, b_ref, o_ref, ob_ref):
    y = _layer_norm_rows(x_ref[...], w_ref[...], b_ref[...])
    o_ref[...] = y
    ob_ref[...] = y.astype(BF16)


def _ln(x2d, w, b):
    n, d = x2d.shape
    tm = min(512, n)
    return pl.pallas_call(
        _ln_kernel,
        grid=(n // tm,),
        in_specs=[pl.BlockSpec((tm, d), lambda i: (i, 0)),
                  pl.BlockSpec((1, d), lambda i: (0, 0)),
                  pl.BlockSpec((1, d), lambda i: (0, 0))],
        out_specs=[pl.BlockSpec((tm, d), lambda i: (i, 0)),
                   pl.BlockSpec((tm, d), lambda i: (i, 0))],
        out_shape=[jax.ShapeDtypeStruct((n, d), F32), jax.ShapeDtypeStruct((n, d), BF16)],
        compiler_params=_cparams(("parallel",), 32),
        name="layer_norm",
    )(x2d, w.reshape(1, d), b.reshape(1, d))


def _mm_kernel(a_ref, w_ref, o_ref):
    o_ref[...] = _dot(a_ref[...], w_ref[...]).astype(o_ref.dtype)


def _mm(a, w, tn, name):
    m, k = a.shape
    n = w.shape[1]
    tm = min(1024, m)
    return pl.pallas_call(
        _mm_kernel,
        grid=(m // tm, n // tn),
        in_specs=[pl.BlockSpec((tm, k), lambda i, j: (i, 0)),
                  pl.BlockSpec((k, tn), lambda i, j: (0, j))],
        out_specs=pl.BlockSpec((tm, tn), lambda i, j: (i, j)),
        out_shape=jax.ShapeDtypeStruct((m, n), F32),
        compiler_params=_cparams(("parallel", "arbitrary"), 40),
        name=name,
    )(a, w)


def _ln_mm_kernel(x_ref, lw_ref, lb_ref, w_ref, o_ref, hb_s):
    @pl.when(pl.program_id(1) == 0)
    def _():
        hb_s[...] = _layer_norm_rows(x_ref[...], lw_ref[...], lb_ref[...]).astype(BF16)

    o_ref[...] = _dot(hb_s[...], w_ref[...])


def _ln_mm(x2d, lw, lb, w, tn):
    m, k = x2d.shape
    n = w.shape[1]
    tm = min(1024, m)
    return pl.pallas_call(
        _ln_mm_kernel,
        grid=(m // tm, n // tn),
        in_specs=[pl.BlockSpec((tm, k), lambda i, j: (i, 0)),
                  pl.BlockSpec((1, k), lambda i, j: (0, 0)),
                  pl.BlockSpec((1, k), lambda i, j: (0, 0)),
                  pl.BlockSpec((k, tn), lambda i, j: (0, j))],
        out_specs=pl.BlockSpec((tm, tn), lambda i, j: (i, j)),
        out_shape=jax.ShapeDtypeStruct((m, n), F32),
        scratch_shapes=[pltpu.VMEM((tm, k), BF16)],
        compiler_params=_cparams(("parallel", "arbitrary"), 56),
        name="ln_in_proj",
    )(x2d, lw.reshape(1, k), lb.reshape(1, k), w)


def _gdn_kernel(q_ref, k_ref, v_ref, z_ref, ab_ref, cwq_ref, cwk_ref, cwv_ref, alog_ref, dtb_ref, nw_ref,
                o_ref, xbuf, s_ref, *, hps, rows):
    C = GDN_CHUNK
    hg = pl.program_id(1)
    tt = pl.program_id(2)

    @pl.when(tt == 0)
    def _():
        xbuf[:, 0:HALO, :] = jnp.zeros((3, HALO, hps * HEAD_DIM), F32)
        s_ref[...] = jnp.zeros_like(s_ref)

    xbuf[0, HALO:HALO + rows, :] = q_ref[...]
    xbuf[1, HALO:HALO + rows, :] = k_ref[...]
    xbuf[2, HALO:HALO + rows, :] = v_ref[...]

    row = lax.broadcasted_iota(jnp.int32, (C, C), 0)
    col = lax.broadcasted_iota(jnp.int32, (C, C), 1)
    tril = row >= col
    strict = row > col
    eye = (row == col).astype(F32)
    tril_f = tril.astype(F32)
    diag16 = strict & ((row // 16) == (col // 16))
    levels = [strict & ((row // (2 * s)) == (col // (2 * s))) & ((row // s) != (col // s)) for s in (16, 32, 64)]
    lane = lax.broadcasted_iota(jnp.int32, (C, HEAD_DIM), 1)
    neg_a = -jnp.exp(alog_ref[...])
    dtb = dtb_ref[...]
    nw = nw_ref[...]
    cws = (cwq_ref[...], cwk_ref[...], cwv_ref[...])

    def conv(which, r0, sl):
        win = xbuf[which, pl.ds(r0, C + HALO), sl]
        cw = cws[which][:, sl]
        y = win[HALO:] * cw[CONV_WIDTH - 1:CONV_WIDTH, :]
        for back in range(1, CONV_WIDTH):
            shifted = pltpu.roll(win, back, 0)[HALO:]
            y = y + shifted * cw[CONV_WIDTH - 1 - back:CONV_WIDTH - back, :]
        return _silu(y)

    def chunk(c, carry):
        r0 = pl.multiple_of(c * C, C)
        ab = ab_ref[pl.ds(r0, C), :]
        g_all = neg_a * _softplus(ab + dtb)
        beta_all = _sigmoid(ab)
        gcum_all = _dot3(tril_f, g_all)
        gcum_t = gcum_all.T
        hs = range(hps)
        sls = [slice(j * HEAD_DIM, (j + 1) * HEAD_DIM) for j in hs]
        heads = [hg * hps + j for j in hs]
        q = [conv(0, r0, sl) for sl in sls]
        k = [conv(1, r0, sl) for sl in sls]
        v = [conv(2, r0, sl) for sl in sls]
        q = [x * lax.rsqrt(jnp.sum(x * x, axis=-1, keepdims=True) + RMS_EPS) * ATTN_SCALE for x in q]
        k = [x * lax.rsqrt(jnp.sum(x * x, axis=-1, keepdims=True) + RMS_EPS) for x in k]
        gcol = [jnp.sum(jnp.where(lane == h, gcum_all, 0.0), axis=1, keepdims=True) for h in heads]
        grow = [jnp.sum(jnp.where(row == h, gcum_t, 0.0), axis=0, keepdims=True) for h in heads]
        beta = [jnp.sum(jnp.where(lane == GDN_HEADS + h, beta_all, 0.0), axis=1, keepdims=True) for h in heads]
        decay = [jnp.where(tril, jnp.exp(jnp.where(tril, gcol[j] - grow[j], 0.0)), 0.0) for j in hs]
        kb = [k[j] * beta[j] for j in hs]
        vb = [v[j] * beta[j] for j in hs]
        kk = _dot3_many(kb, [x.T for x in k])
        low = [jnp.where(strict, kk[j] * decay[j], 0.0) for j in hs]
        a = [jnp.where(diag16, -low[j], 0.0) for j in hs]
        inv = [eye + a[j] for j in hs]
        for _ in range(3):
            a = _dot3_many(a, a)
            t = _dot3_many(inv, a)
            inv = [inv[j] + t[j] for j in hs]
        for lv in levels:
            t = _dot3_many([jnp.where(lv, low[j], 0.0) for j in hs], inv)
            t = _dot3_many(inv, t)
            inv = [inv[j] - t[j] for j in hs]
        eg = [jnp.exp(gcol[j]) for j in hs]
        uw = _dot3_many(inv, [jnp.concatenate([vb[j], kb[j] * eg[j]], axis=1) for j in hs])
        qk = [_dot_nt(q[j].astype(BF16), k[j].astype(BF16)) for j in hs]
        qk = [jnp.where(tril, qk[j] * decay[j], 0.0).astype(BF16) for j in hs]
        s = [s_ref[j] for j in hs]
        sb = [x.astype(BF16) for x in s]
        ws = [_dot(uw[j][:, HEAD_DIM:].astype(BF16), sb[j]) for j in hs]
        o1 = [_dot((q[j] * eg[j]).astype(BF16), sb[j]) for j in hs]
        vnb = [(uw[j][:, :HEAD_DIM] - ws[j]).astype(BF16) for j in hs]
        o2 = [_dot(qk[j], vnb[j]) for j in hs]
        g_last = [gcol[j][C - 1:C, :] for j in hs]
        kdt = [(k[j] * jnp.exp(g_last[j] - gcol[j])).T.astype(BF16) for j in hs]
        upd = [_dot(kdt[j], vnb[j]) for j in hs]
        for j in hs:
            s_ref[j] = s[j] * jnp.exp(g_last[j]) + upd[j]
            z = z_ref[pl.ds(r0, C), sls[j]]
            o_ref[pl.ds(r0, C), sls[j]] = (_rms_rows(o1[j] + o2[j], nw) * _silu(z)).astype(o_ref.dtype)
        return carry

    lax.fori_loop(0, rows // C, chunk, 0)
    xbuf[:, 0:HALO, :] = xbuf[:, rows:rows + HALO, :]


def _gdn(proj, conv_w, a_log, dt_bias, gdn_norm_w, batch, seq):
    hps = 8
    rows = min(512, seq)
    nt = seq // rows
    w = hps * HEAD_DIM
    cwq, cwk, cwv = (conv_w[:, i * GDN_WIDTH:(i + 1) * GDN_WIDTH] for i in range(3))
    pad = lambda t: jnp.zeros((1, 128), F32).at[0, :GDN_HEADS].set(t)

    def col(base):
        return lambda b, g, t: (b * nt + t, base // w + g)

    return pl.pallas_call(
        functools.partial(_gdn_kernel, hps=hps, rows=rows),
        grid=(batch, GDN_HEADS // hps, nt),
        in_specs=[pl.BlockSpec((rows, w), col(COL_GQ)),
                  pl.BlockSpec((rows, w), col(COL_GK)),
                  pl.BlockSpec((rows, w), col(COL_GV)),
                  pl.BlockSpec((rows, w), col(COL_GZ)),
                  pl.BlockSpec((rows, 128), lambda b, g, t: (b * nt + t, COL_AB // 128)),
                  pl.BlockSpec((CONV_WIDTH, w), lambda b, g, t: (0, g)),
                  pl.BlockSpec((CONV_WIDTH, w), lambda b, g, t: (0, g)),
                  pl.BlockSpec((CONV_WIDTH, w), lambda b, g, t: (0, g)),
                  pl.BlockSpec((1, 128), lambda b, g, t: (0, 0)),
                  pl.BlockSpec((1, 128), lambda b, g, t: (0, 0)),
                  pl.BlockSpec((1, 128), lambda b, g, t: (0, 0))],
        out_specs=pl.BlockSpec((rows, w), lambda b, g, t: (b * nt + t, g)),
        out_shape=jax.ShapeDtypeStruct((batch * seq, GDN_WIDTH), BF16),
        scratch_shapes=[pltpu.VMEM((3, rows + HALO, w), F32), pltpu.VMEM((hps, HEAD_DIM, HEAD_DIM), F32)],
        compiler_params=_cparams(("parallel", "parallel", "arbitrary"), 32),
        name="gated_deltanet",
    )(proj, proj, proj, proj, proj, cwq, cwk, cwv, pad(a_log), pad(dt_bias), gdn_norm_w.reshape(1, 128))


def _sb_kernel(q_ref, k_ref, v_ref, nw_ref, o_ref, *, tq, tk, hps):
    qi = pl.program_id(2)
    q0 = qi * tq
    hs = range(hps)
    sls = [slice(j * HEAD_DIM, (j + 1) * HEAD_DIM) for j in hs]
    q = [q_ref[:, sl].astype(BF16) for sl in sls]
    rowi = lax.broadcasted_iota(jnp.int32, (tq, tk), 0)
    coli = lax.broadcasted_iota(jnp.int32, (tq, tk), 1)
    jj = lax.broadcasted_iota(jnp.int32, (tk, tk), 0)
    ss = lax.broadcasted_iota(jnp.int32, (tk, tk), 1)
    later = (jj > ss).astype(BF16)
    nk = (qi + 1) * (tq // tk)

    def cond(carry):
        i, _, _, live = carry
        return jnp.logical_and(i < nk, live > SB_LOG_CUTOFF)

    def body(carry):
        i, acc, run, _ = carry
        k0 = pl.multiple_of((nk - 1 - i) * tk, tk)
        causal = (coli + k0) < (rowi + q0)
        kb = [k_ref[pl.ds(k0, tk), sl].astype(BF16) for sl in sls]
        vb = [v_ref[pl.ds(k0, tk), sl].astype(BF16) for sl in sls]
        z = [_dot_nt(q[j], kb[j]) * ATTN_SCALE for j in hs]
        sp = [_softplus(z[j]) for j in hs]
        log_not = [jnp.where(causal, -sp[j], 0.0) for j in hs]
        parts = [_split(log_not[j]) for j in hs]
        both = [_dot(jnp.concatenate(parts[j], axis=0), later) for j in hs]
        after = [both[j][:tq] + both[j][tq:] + run[j] for j in hs]
        a = [jnp.where(causal, jnp.exp((z[j] - sp[j]) + after[j]), 0.0).astype(BF16) for j in hs]
        acc = tuple(acc[j] + _dot(a[j], vb[j]) for j in hs)
        run = tuple(run[j] + jnp.sum(log_not[j], axis=1, keepdims=True) for j in hs)
        live = jnp.max(run[0])
        for j in range(1, hps):
            live = jnp.maximum(live, jnp.max(run[j]))
        return i + 1, acc, run, live

    init = (jnp.int32(0), tuple(jnp.zeros((tq, HEAD_DIM), F32) for _ in hs),
            tuple(jnp.zeros((tq, 1), F32) for _ in hs), jnp.float32(0.0))
    _, acc, _, _ = lax.while_loop(cond, body, init)
    for j in hs:
        o_ref[:, sls[j]] = _rms_rows(acc[j], nw_ref[:, sls[j]]).astype(o_ref.dtype)


def _sb(proj, sb_norm_w, batch, seq):
    tq = min(256, seq)
    tk = 128
    hps = 4
    w = hps * HEAD_DIM
    nq = seq // tq
    cq, ck, cv = COL_SQ // w, COL_SK // w, COL_SV // w
    return pl.pallas_call(
        functools.partial(_sb_kernel, tq=tq, tk=tk, hps=hps),
        grid=(batch, SB_HEADS // hps, nq),
        in_specs=[pl.BlockSpec((tq, w), lambda b, h, i: (b * nq + i, cq + h)),
                  pl.BlockSpec((seq, w), lambda b, h, i: (b, ck + h)),
                  pl.BlockSpec((seq, w), lambda b, h, i: (b, cv + h)),
                  pl.BlockSpec((1, w), lambda b, h, i: (0, h))],
        out_specs=pl.BlockSpec((tq, w), lambda b, h, i: (b * nq + i, h)),
        out_shape=jax.ShapeDtypeStruct((batch * seq, SB_WIDTH), BF16),
        compiler_params=_cparams(("parallel", "parallel", "arbitrary"), 48),
        name="stick_breaking",
    )(proj, proj, proj, sb_norm_w.reshape(1, SB_WIDTH))


def _mem_kernel(q_ref, kv_ref, nw_ref, o_ref):
    for h in range(MEM_HEADS):
        sl = slice(h * HEAD_DIM, (h + 1) * HEAD_DIM)
        q = q_ref[:, sl].astype(BF16)
        k = kv_ref[:, sl].astype(BF16)
        v = kv_ref[:, MEM_WIDTH + h * HEAD_DIM:MEM_WIDTH + (h + 1) * HEAD_DIM].astype(BF16)
        s = _dot_nt(q, k) * ATTN_SCALE
        e = jnp.exp(s - jnp.max(s, axis=-1, keepdims=True))
        p = e / jnp.sum(e, axis=-1, keepdims=True)
        o = _dot(p.astype(BF16), v)
        o_ref[:, sl] = _rms_rows(o, nw_ref[:, sl]).astype(o_ref.dtype)


def _mem_attn(proj, kv, mem_norm_w, batch, seq, n_mem):
    tq = min(512, seq)
    nq = seq // tq
    return pl.pallas_call(
        _mem_kernel,
        grid=(batch, nq),
        in_specs=[pl.BlockSpec((tq, MEM_WIDTH), lambda b, i: (b * nq + i, COL_MQ // MEM_WIDTH)),
                  pl.BlockSpec((n_mem, 2 * MEM_WIDTH), lambda b, i: (b, 0)),
                  pl.BlockSpec((1, MEM_WIDTH), lambda b, i: (0, 0))],
        out_specs=pl.BlockSpec((tq, MEM_WIDTH), lambda b, i: (b * nq + i, 0)),
        out_shape=jax.ShapeDtypeStruct((batch * seq, MEM_WIDTH), BF16),
        compiler_params=_cparams(("parallel", "parallel"), 32),
        name="memory_attention",
    )(proj, kv, mem_norm_w.reshape(1, MEM_WIDTH))


def _out_ln_kernel(og_ref, os_ref, om_ref, wg_ref, ws_ref, wm_ref, x_ref, ew_ref, eb_ref, lw_ref, lb_ref,
                   o_ref, ob_ref):
    mix = _dot(og_ref[...], wg_ref[...]) + _dot(os_ref[...], ws_ref[...]) + _dot(om_ref[...], wm_ref[...])
    res = _layer_norm_rows(x_ref[...], ew_ref[...], eb_ref[...])
    y = _layer_norm_rows(DEEPNORM_ALPHA * res + mix, lw_ref[...], lb_ref[...])
    o_ref[...] = y
    ob_ref[...] = y.astype(BF16)


def _out_ln(o_gdn, o_sb, o_mem, w_out, x2d, ew, eb, lw, lb):
    n, d = x2d.shape
    tm = min(256, n)
    wg, ws, wm = w_out[:GDN_WIDTH], w_out[GDN_WIDTH:GDN_WIDTH + SB_WIDTH], w_out[GDN_WIDTH + SB_WIDTH:]
    rowb = lambda width: pl.BlockSpec((tm, width), lambda i: (i, 0))
    full = lambda r, c: pl.BlockSpec((r, c), lambda i: (0, 0))
    return pl.pallas_call(
        _out_ln_kernel,
        grid=(n // tm,),
        in_specs=[rowb(GDN_WIDTH), rowb(SB_WIDTH), rowb(MEM_WIDTH),
                  full(GDN_WIDTH, d), full(SB_WIDTH, d), full(MEM_WIDTH, d),
                  rowb(d), full(1, d), full(1, d), full(1, d), full(1, d)],
        out_specs=[rowb(d), rowb(d)],
        out_shape=[jax.ShapeDtypeStruct((n, d), F32), jax.ShapeDtypeStruct((n, d), BF16)],
        compiler_params=_cparams(("parallel",), 60),
        name="out_proj_ln",
    )(o_gdn, o_sb, o_mem, wg, ws, wm, x2d, ew.reshape(1, d), eb.reshape(1, d), lw.reshape(1, d), lb.reshape(1, d))


def _peer_score_kernel(h_ref, wq_ref, k1_ref, k2_ref, s1_ref, s2_ref):
    q = _dot(h_ref[...], wq_ref[...]).astype(BF16)
    k1 = k1_ref[...]
    k2 = k2_ref[...]
    for h in range(PEER_HEADS):
        base = h * 2 * PEER_D_HALF
        s1_ref[h] = _dot_nt(k1, q[:, base:base + PEER_D_HALF])
        s2_ref[h] = _dot_nt(k2, q[:, base + PEER_D_HALF:base + 2 * PEER_D_HALF])


def _peer_scores(hb, wq, keys1, keys2):
    n, d = hb.shape
    tb = min(512, n)
    dq = wq.shape[1]
    out = jax.ShapeDtypeStruct((PEER_HEADS, PEER_N_KEYS, n), F32)
    ospec = pl.BlockSpec((PEER_HEADS, PEER_N_KEYS, tb), lambda i: (0, 0, i))
    return pl.pallas_call(
        _peer_score_kernel,
        grid=(n // tb,),
        in_specs=[pl.BlockSpec((tb, d), lambda i: (i, 0)),
                  pl.BlockSpec((d, dq), lambda i: (0, 0)),
                  pl.BlockSpec((PEER_N_KEYS, PEER_D_HALF), lambda i: (0, 0)),
                  pl.BlockSpec((PEER_N_KEYS, PEER_D_HALF), lambda i: (0, 0))],
        out_specs=[ospec, ospec],
        out_shape=[out, out],
        compiler_params=_cparams(("parallel",), 58),
        name="peer_scores",
    )(hb, wq, keys1, keys2)


NTOP = PEER_TOPK + 1
VROWS = 24


def _extract_top(cur, n, rowid, out_ref=None):
    vals = []
    nrows = cur.shape[0]
    for it in range(n):
        m = jnp.max(cur, axis=0, keepdims=True)
        first = jnp.min(jnp.where(cur == m, rowid, float(nrows)), axis=0, keepdims=True)
        cur = jnp.where(rowid == first, NEG_BIG, cur)
        vals.append(m)
        if out_ref is not None:
            out_ref[it:it + 1, :] = m
    return vals


def _extract_top_distinct(cur, n, out_ref):
    for it in range(n):
        m = jnp.max(cur, axis=0, keepdims=True)
        cur = jnp.where(cur == m, NEG_BIG, cur)
        out_ref[it:it + 1, :] = m
    return jnp.sum(jnp.where(cur == NEG_BIG, 1.0, 0.0), axis=0, keepdims=True)


def _peer_gate_kernel(s1_ref, s2_ref, t1_ref, d1_ref, e2_ref, v1_s, v2_s, top_s):
    s1 = s1_ref[0]
    s2 = s2_ref[0]
    tb = s1.shape[1]
    v1_s[...] = jnp.full(v1_s.shape, NEG_BIG, F32)
    v2_s[...] = jnp.full(v2_s.shape, NEG_BIG, F32)
    lost = jnp.maximum(_extract_top_distinct(s1, NTOP, v1_s), _extract_top_distinct(s2, NTOP, v2_s))

    @pl.when(jnp.max(lost) > NTOP)
    def _():
        rowid = lax.broadcasted_iota(jnp.int32, s1.shape, 0).astype(F32)
        _extract_top(s1, NTOP, rowid, v1_s)
        _extract_top(s2, NTOP, rowid, v2_s)

    v1 = [v1_s[a:a + 1, :] for a in range(8)]
    r8 = lax.broadcasted_iota(jnp.int32, (8, tb), 0)
    pieces = [v1[0] + v2_s[...]]
    pads = 2 * (VROWS - NTOP)
    for a in range(1, 8):
        nb = NTOP // (a + 1)
        pads += 8 - nb
        pieces.append(jnp.where(r8 < nb, v1[a] + v2_s[0:8, :], NEG_BIG))
    pieces.append(v1_s[8:VROWS, :] + v2_s[0:1, :])
    cand = jnp.concatenate(pieces, axis=0)
    lost = _extract_top_distinct(cand, NTOP, top_s)

    @pl.when(jnp.max(lost) > NTOP + pads)
    def _():
        candid = lax.broadcasted_iota(jnp.int32, cand.shape, 0).astype(F32)
        _extract_top(cand, NTOP, candid, top_s)

    top0 = top_s[0:1, :]
    thr = 0.5 * (top_s[PEER_TOPK - 1:PEER_TOPK, :] + top_s[PEER_TOPK:PEER_TOPK + 1, :])
    zsum = jnp.sum(jnp.where(cand >= thr, jnp.exp(cand - top0), 0.0), axis=0, keepdims=True)
    m2 = v2_s[0:1, :]
    t1_ref[0] = jnp.exp((thr - s1) - m2)
    d1_ref[0] = jnp.exp(s1 - v1[0]) / zsum
    e2_ref[0] = jnp.exp(s2 - m2)


def _peer_gates(s1t, s2t):
    nh, nk, n = s1t.shape
    tb = min(512, n)
    spec = pl.BlockSpec((1, nk, tb), lambda h, i: (h, 0, i))
    out = jax.ShapeDtypeStruct((nh, nk, n), F32)
    return pl.pallas_call(
        _peer_gate_kernel,
        grid=(nh, n // tb),
        in_specs=[spec, spec],
        out_specs=[spec, spec, spec],
        out_shape=[out, out, out],
        scratch_shapes=[pltpu.VMEM((VROWS, tb), F32)] * 3,
        compiler_params=_cparams(("parallel", "parallel"), 32),
        name="peer_gates",
    )(s1t, s2t)


SQRT_HALF = 0.7071067811865476
PEER_GROUP = 8
PEER_LANES = 256


def _gelu(x):
    return 0.5 * x * (1.0 + lax.erf(x * SQRT_HALF))


def _peer_mix_kernel(x_ref, u_ref, vt_ref, t1_ref, d1_ref, e2_ref, o_ref, *scratch):
    nchunk = len(scratch) // 2
    act_refs = scratch[:nchunk]
    p_refs = scratch[nchunk:]
    g = pl.program_id(1)
    nk = PEER_N_KEYS

    def cols(c):
        return slice(c * PEER_LANES, (c + 1) * PEER_LANES)

    def scores(c):
        act_refs[c][...] = _dot_nt(u_ref[...], x_ref[cols(c), :])

    def weigh(c, i):
        cs = cols(c)
        rs = slice(i * nk, (i + 1) * nk)
        w = jnp.zeros((nk, PEER_LANES), F32)
        for h in range(PEER_HEADS):
            e2 = e2_ref[h, :, cs]
            w = w + jnp.where(e2 >= t1_ref[h, i:i + 1, cs], e2, 0.0) * d1_ref[h, i:i + 1, cs]
        p_refs[c][rs, :] = (_gelu(act_refs[c][rs, :]) * w).astype(BF16)

    def mix(c, pair, assign):
        rs = slice(pair * 2 * nk, (pair + 1) * 2 * nk)
        upd = _dot(vt_ref[:, rs], p_refs[c][rs, :])
        if assign:
            o_ref[:, cols(c)] = upd
        else:
            o_ref[:, cols(c)] += upd

    def step(first_group):
        scores(0)
        for c in range(nchunk):
            if c + 1 < nchunk:
                scores(c + 1)
            for pair in range(PEER_GROUP // 2):
                weigh(c, 2 * pair)
                weigh(c, 2 * pair + 1)
                if pair >= 1:
                    mix(c, pair - 1, first_group and pair == 1)
            mix(c, PEER_GROUP // 2 - 1, False)

    @pl.when(g == 0)
    def _():
        step(True)

    @pl.when(g != 0)
    def _():
        step(False)


def _peer_mix(hb, u_b, vt_b, t1t, d1t, e2t):
    n, d = hb.shape
    tb = min(1024, n)
    ne = u_b.shape[0]
    eb = PEER_GROUP * PEER_N_KEYS
    ng = ne // eb
    gspec = pl.BlockSpec((PEER_HEADS, PEER_GROUP, tb), lambda t, g: (0, g, t))
    kspec = pl.BlockSpec((PEER_HEADS, PEER_N_KEYS, tb), lambda t, g: (0, 0, t))
    nchunk = tb // PEER_LANES
    return pl.pallas_call(
        _peer_mix_kernel,
        grid=(n // tb, ng),
        in_specs=[pl.BlockSpec((tb, d), lambda t, g: (t, 0)),
                  pl.BlockSpec((eb, d), lambda t, g: (g, 0)),
                  pl.BlockSpec((d, eb), lambda t, g: (0, g)),
                  gspec, gspec, kspec],
        out_specs=pl.BlockSpec((d, tb), lambda t, g: (0, t)),
        out_shape=jax.ShapeDtypeStruct((d, n), F32),
        scratch_shapes=([pltpu.VMEM((eb, PEER_LANES), F32)] * nchunk
                        + [pltpu.VMEM((eb, PEER_LANES), BF16)] * nchunk),
        compiler_params=_cparams(("parallel", "arbitrary"), 60),
        name="peer_mix",
    )(hb, u_b, vt_b, t1t, d1t, e2t)


def _final_ln_kernel(h_ref, ft_ref, w_ref, b_ref, o_ref):
    y = DEEPNORM_ALPHA * h_ref[...] + ft_ref[...].T
    o_ref[...] = _layer_norm_rows(y, w_ref[...], b_ref[...])


def _final_ln(h, ffn_t, w, b):
    n, d = h.shape
    tm = min(512, n)
    return pl.pallas_call(
        _final_ln_kernel,
        grid=(n // tm,),
        in_specs=[pl.BlockSpec((tm, d), lambda i: (i, 0)),
                  pl.BlockSpec((d, tm), lambda i: (0, i)),
                  pl.BlockSpec((1, d), lambda i: (0, 0)),
                  pl.BlockSpec((1, d), lambda i: (0, 0))],
        out_specs=pl.BlockSpec((tm, d), lambda i: (i, 0)),
        out_shape=jax.ShapeDtypeStruct((n, d), F32),
        compiler_params=_cparams(("parallel",), 40),
        name="final_ln",
    )(h, ffn_t, w.reshape(1, d), b.reshape(1, d))


def _reorder_w_in(w_in):
    o_ab = 4 * GDN_WIDTH
    d = w_in.shape[0]
    main = jnp.concatenate([w_in[:, :o_ab], w_in[:, o_ab + 2 * GDN_HEADS:]], axis=1)
    ab = jnp.zeros((d, PROJ_COLS - COL_AB), w_in.dtype).at[:, :2 * GDN_HEADS].set(w_in[:, o_ab:o_ab + 2 * GDN_HEADS])
    return jnp.concatenate([main, ab], axis=1)


def kernel(x, mem, ln_emb_w, ln_emb_b, w_in, conv_w, a_log, dt_bias, gdn_norm_w, sb_norm_w,
           mem_norm_w, ln_mem_w, ln_mem_b, w_mem_kv, w_out, ln1_w, ln1_b, peer_wq, peer_keys1,
           peer_keys2, peer_u, peer_v, ln2_w, ln2_b):
    batch, seq, d = x.shape
    n_mem = mem.shape[1]
    l = 0
    x2d = x.reshape(batch * seq, d)
    proj = _ln_mm(x2d, ln_emb_w, ln_emb_b, _reorder_w_in(w_in[l].astype(BF16)), PROJ_TILE)

    o_gdn = _gdn(proj, conv_w[l], a_log[l], dt_bias[l], gdn_norm_w[l], batch, seq)
    o_sb = _sb(proj, sb_norm_w[l], batch, seq)
    _, mb = _ln(mem.reshape(batch * n_mem, d), ln_mem_w[l], ln_mem_b[l])
    kv = _mm(mb, w_mem_kv[l].astype(BF16), 2 * MEM_WIDTH // 2, "mem_kv")
    o_mem = _mem_attn(proj, kv, mem_norm_w[l], batch, seq, n_mem)

    h1, h1b = _out_ln(o_gdn, o_sb, o_mem, w_out[l].astype(BF16), x2d, ln_emb_w, ln_emb_b, ln1_w[l], ln1_b[l])

    s1t, s2t = _peer_scores(h1b, peer_wq[l].astype(BF16), peer_keys1[l].astype(BF16), peer_keys2[l].astype(BF16))
    t1t, d1t, e2t = _peer_gates(s1t, s2t)
    ffn_t = _peer_mix(h1b, peer_u[l].astype(BF16), peer_v[l].astype(BF16).T, t1t, d1t, e2t)
    out = _final_ln(h1, ffn_t, ln2_w[l], ln2_b[l])
    return out.reshape(batch, seq, d)
```

```python
import functools

import jax
import jax.numpy as jnp
from jax import lax
from jax.experimental import pallas as pl
from jax.experimental.pallas import tpu as pltpu

F32 = jnp.float32
BF16 = jnp.bfloat16

HEAD_DIM = 128
GDN_HEADS = 8
SB_HEADS = 4
MEM_HEADS = 4
GDN_WIDTH = GDN_HEADS * HEAD_DIM
SB_WIDTH = SB_HEADS * HEAD_DIM
MEM_WIDTH = MEM_HEADS * HEAD_DIM
CONV_WIDTH = 4
PEER_HEADS = 8
PEER_N_KEYS = 128
PEER_TOPK = 16
PEER_D_HALF = 128
DEPTH = 1
DEEPNORM_ALPHA = (2 * DEPTH) ** 0.25
LN_EPS = 1e-5
RMS_EPS = 1e-6
ATTN_SCALE = HEAD_DIM ** -0.5

COL_GQ = 0
COL_GK = GDN_WIDTH
COL_GV = 2 * GDN_WIDTH
COL_GZ = 3 * GDN_WIDTH
COL_SQ = 4 * GDN_WIDTH
COL_SK = COL_SQ + SB_WIDTH
COL_SV = COL_SK + SB_WIDTH
COL_MQ = COL_SV + SB_WIDTH
COL_AB = COL_MQ + MEM_WIDTH
MXU_WIDTH = 256
PROJ_TILE = 5 * MXU_WIDTH
PROJ_COLS = -(-(COL_AB + 128) // PROJ_TILE) * PROJ_TILE

GDN_CHUNK = 128
HALO = 8
NEG_BIG = -3.0e38
SB_LOG_CUTOFF = -104.0
MIB = 1024 * 1024


def _cparams(sem, vmem_mib):
    return pltpu.CompilerParams(dimension_semantics=sem, vmem_limit_bytes=vmem_mib * MIB)


def _dot(a, b):
    return jnp.dot(a, b, preferred_element_type=F32)


def _dot_nt(a, b):
    return lax.dot_general(a, b, (((1,), (1,)), ((), ())), preferred_element_type=F32)


def _split(a):
    hi = a.astype(BF16)
    lo = (a - hi.astype(F32)).astype(BF16)
    return hi, lo


def _dot3(a, b):
    return _dot3_many([a], [b])[0]


def _dot3_many(as_, bs_):
    sa = [_split(a) for a in as_]
    sb = [_split(b) for b in bs_]
    both = [_dot(jnp.concatenate(x, axis=0), y[0]) for x, y in zip(sa, sb)]
    hl = [_dot(x[0], y[1]) for x, y in zip(sa, sb)]
    return [t[:a.shape[0]] + (t[a.shape[0]:] + m) for a, t, m in zip(as_, both, hl)]


def _softplus(x):
    return jnp.maximum(x, 0.0) + jnp.log(1.0 + jnp.exp(-jnp.abs(x)))


def _sigmoid(x):
    return 1.0 / (1.0 + jnp.exp(-x))


def _silu(x):
    return x * _sigmoid(x)


def _layer_norm_rows(x, w, b):
    mu = jnp.mean(x, axis=-1, keepdims=True)
    xc = x - mu
    var = jnp.mean(xc * xc, axis=-1, keepdims=True)
    return xc * lax.rsqrt(var + LN_EPS) * w + b


def _rms_rows(x, w):
    return x * lax.rsqrt(jnp.mean(x * x, axis=-1, keepdims=True) + RMS_EPS) * w


def _ln_kernel(x_ref, w_ref, b_ref, o_ref, ob_ref):
    y = _layer_norm_rows(x_ref[...], w_ref[...], b_ref[...])
    o_ref[...] = y
    ob_ref[...] = y.astype(BF16)


def _ln(x2d, w, b):
    n, d = x2d.shape
    tm = min(512, n)
    return pl.pallas_call(
        _ln_kernel,
        grid=(n // tm,),
        in_specs=[pl.BlockSpec((tm, d), lambda i: (i, 0)),
                  pl.BlockSpec((1, d), lambda i: (0, 0)),
                  pl.BlockSpec((1, d), lambda i: (0, 0))],
        out_specs=[pl.BlockSpec((tm, d), lambda i: (i, 0)),
                   pl.BlockSpec((tm, d), lambda i: (i, 0))],
        out_shape=[jax.ShapeDtypeStruct((n, d), F32), jax.ShapeDtypeStruct((n, d), BF16)],
        compiler_params=_cparams(("parallel",), 32),
        name="layer_norm",
    )(x2d, w.reshape(1, d), b.reshape(1, d))


def _mm_kernel(a_ref, w_ref, o_ref):
    o_ref[...] = _dot(a_ref[...], w_ref[...]).astype(o_ref.dtype)


def _mm(a, w, tn, name):
    m, k = a.shape
    n = w.shape[1]
    tm = min(1024, m)
    return pl.pallas_call(
        _mm_kernel,
        grid=(m // tm, n // tn),
        in_specs=[pl.BlockSpec((tm, k), lambda i, j: (i, 0)),
                  pl.BlockSpec((k, tn), lambda i, j: (0, j))],
        out_specs=pl.BlockSpec((tm, tn), lambda i, j: (i, j)),
        out_shape=jax.ShapeDtypeStruct((m, n), F32),
        compiler_params=_cparams(("parallel", "arbitrary"), 40),
        name=name,
    )(a, w)


def _ln_mm_kernel(x_ref, lw_ref, lb_ref, w_ref, o_ref, hb_s):
    @pl.when(pl.program_id(1) == 0)
    def _():
        hb_s[...] = _layer_norm_rows(x_ref[...], lw_ref[...], lb_ref[...]).astype(BF16)

    o_ref[...] = _dot(hb_s[...], w_ref[...])


def _ln_mm(x2d, lw, lb, w, tn):
    m, k = x2d.shape
    n = w.shape[1]
    tm = min(1024, m)
    return pl.pallas_call(
        _ln_mm_kernel,
        grid=(m // tm, n // tn),
        in_specs=[pl.BlockSpec((tm, k), lambda i, j: (i, 0)),
                  pl.BlockSpec((1, k), lambda i, j: (0, 0)),
                  pl.BlockSpec((1, k), lambda i, j: (0, 0)),
                  pl.BlockSpec((k, tn), lambda i, j: (0, j))],
        out_specs=pl.BlockSpec((tm, tn), lambda i, j: (i, j)),
        out_shape=jax.ShapeDtypeStruct((m, n), F32),
        scratch_shapes=[pltpu.VMEM((tm, k), BF16)],
        compiler_params=_cparams(("parallel", "arbitrary"), 56),
        name="ln_in_proj",
    )(x2d, lw.reshape(1, k), lb.reshape(1, k), w)


def _gdn_kernel(q_ref, k_ref, v_ref, z_ref, ab_ref, cwq_ref, cwk_ref, cwv_ref, alog_ref, dtb_ref, nw_ref,
                o_ref, xbuf, s_ref, *, hps, rows):
    C = GDN_CHUNK
    hg = pl.program_id(1)
    tt = pl.program_id(2)

    @pl.when(tt == 0)
    def _():
        xbuf[:, 0:HALO, :] = jnp.zeros((3, HALO, hps * HEAD_DIM), F32)
        s_ref[...] = jnp.zeros_like(s_ref)

    xbuf[0, HALO:HALO + rows, :] = q_ref[...]
    xbuf[1, HALO:HALO + rows, :] = k_ref[...]
    xbuf[2, HALO:HALO + rows, :] = v_ref[...]

    row = lax.broadcasted_iota(jnp.int32, (C, C), 0)
    col = lax.broadcasted_iota(jnp.int32, (C, C), 1)
    tril = row >= col
    strict = row > col
    eye = (row == col).astype(F32)
    tril_f = tril.astype(F32)
    diag16 = strict & ((row // 16) == (col // 16))
    levels = [strict & ((row // (2 * s)) == (col // (2 * s))) & ((row // s) != (col // s)) for s in (16, 32, 64)]
    lane = lax.broadcasted_iota(jnp.int32, (C, HEAD_DIM), 1)
    neg_a = -jnp.exp(alog_ref[...])
    dtb = dtb_ref[...]
    nw = nw_ref[...]
    cws = (cwq_ref[...], cwk_ref[...], cwv_ref[...])

    def conv(which, r0, sl):
        win = xbuf[which, pl.ds(r0, C + HALO), sl]
        cw = cws[which][:, sl]
        y = win[HALO:] * cw[CONV_WIDTH - 1:CONV_WIDTH, :]
        for back in range(1, CONV_WIDTH):
            shifted = pltpu.roll(win, back, 0)[HALO:]
            y = y + shifted * cw[CONV_WIDTH - 1 - back:CONV_WIDTH - back, :]
        return _silu(y)

    def chunk(c, carry):
        r0 = pl.multiple_of(c * C, C)
        ab = ab_ref[pl.ds(r0, C), :]
        g_all = neg_a * _softplus(ab + dtb)
        beta_all = _sigmoid(ab)
        gcum_all = _dot3(tril_f, g_all)
        gcum_t = gcum_all.T
        hs = range(hps)
        sls = [slice(j * HEAD_DIM, (j + 1) * HEAD_DIM) for j in hs]
        heads = [hg * hps + j for j in hs]
        q = [conv(0, r0, sl) for sl in sls]
        k = [conv(1, r0, sl) for sl in sls]
        v = [conv(2, r0, sl) for sl in sls]
        q = [x * lax.rsqrt(jnp.sum(x * x, axis=-1, keepdims=True) + RMS_EPS) * ATTN_SCALE for x in q]
        k = [x * lax.rsqrt(jnp.sum(x * x, axis=-1, keepdims=True) + RMS_EPS) for x in k]
        gcol = [jnp.sum(jnp.where(lane == h, gcum_all, 0.0), axis=1, keepdims=True) for h in heads]
        grow = [jnp.sum(jnp.where(row == h, gcum_t, 0.0), axis=0, keepdims=True) for h in heads]
        beta = [jnp.sum(jnp.where(lane == GDN_HEADS + h, beta_all, 0.0), axis=1, keepdims=True) for h in heads]
        decay = [jnp.where(tril, jnp.exp(jnp.where(tril, gcol[j] - grow[j], 0.0)), 0.0) for j in hs]
        kb = [k[j] * beta[j] for j in hs]
        vb = [v[j] * beta[j] for j in hs]
        kk = _dot3_many(kb, [x.T for x in k])
        low = [jnp.where(strict, kk[j] * decay[j], 0.0) for j in hs]
        a = [jnp.where(diag16, -low[j], 0.0) for j in hs]
        inv = [eye + a[j] for j in hs]
        a = _dot3_many(a, a)
        for _ in range(2):
            both = _dot3_many([jnp.concatenate([inv[j], a[j]], axis=0) for j in hs], a)
            inv = [inv[j] + both[j][:C] for j in hs]
            a = [both[j][C:] for j in hs]
        t = _dot3_many(inv, a)
        inv = [inv[j] + t[j] for j in hs]
        for lv in levels:
            t = _dot3_many([jnp.where(lv, low[j], 0.0) for j in hs], inv)
            t = _dot3_many(inv, t)
            inv = [inv[j] - t[j] for j in hs]
        eg = [jnp.exp(gcol[j]) for j in hs]
        uw = _dot3_many(inv, [jnp.concatenate([vb[j], kb[j] * eg[j]], axis=1) for j in hs])
        qk = [_dot_nt(q[j].astype(BF16), k[j].astype(BF16)) for j in hs]
        qk = [jnp.where(tril, qk[j] * decay[j], 0.0).astype(BF16) for j in hs]
        s = [s_ref[j] for j in hs]
        sb = [x.astype(BF16) for x in s]
        ws = [_dot(uw[j][:, HEAD_DIM:].astype(BF16), sb[j]) for j in hs]
        o1 = [_dot((q[j] * eg[j]).astype(BF16), sb[j]) for j in hs]
        vnb = [(uw[j][:, :HEAD_DIM] - ws[j]).astype(BF16) for j in hs]
        o2 = [_dot(qk[j], vnb[j]) for j in hs]
        g_last = [gcol[j][C - 1:C, :] for j in hs]
        kdt = [(k[j] * jnp.exp(g_last[j] - gcol[j])).T.astype(BF16) for j in hs]
        upd = [_dot(kdt[j], vnb[j]) for j in hs]
        for j in hs:
            s_ref[j] = s[j] * jnp.exp(g_last[j]) + upd[j]
            z = z_ref[pl.ds(r0, C), sls[j]]
            o_ref[pl.ds(r0, C), sls[j]] = (_rms_rows(o1[j] + o2[j], nw) * _silu(z)).astype(o_ref.dtype)
        return carry

    lax.fori_loop(0, rows // C, chunk, 0)
    xbuf[:, 0:HALO, :] = xbuf[:, rows:rows + HALO, :]


def _gdn(proj, conv_w, a_log, dt_bias, gdn_norm_w, batch, seq):
    hps = 8
    rows = min(512, seq)
    nt = seq // rows
    w = hps * HEAD_DIM
    cwq, cwk, cwv = (conv_w[:, i * GDN_WIDTH:(i + 1) * GDN_WIDTH] for i in range(3))
    pad = lambda t: jnp.zeros((1, 128), F32).at[0, :GDN_HEADS].set(t)

    def col(base):
        return lambda b, g, t: (b * nt + t, base // w + g)

    return pl.pallas_call(
        functools.partial(_gdn_kernel, hps=hps, rows=rows),
        grid=(batch, GDN_HEADS // hps, nt),
        in_specs=[pl.BlockSpec((rows, w), col(COL_GQ)),
                  pl.BlockSpec((rows, w), col(COL_GK)),
                  pl.BlockSpec((rows, w), col(COL_GV)),
                  pl.BlockSpec((rows, w), col(COL_GZ)),
                  pl.BlockSpec((rows, 128), lambda b, g, t: (b * nt + t, COL_AB // 128)),
                  pl.BlockSpec((CONV_WIDTH, w), lambda b, g, t: (0, g)),
                  pl.BlockSpec((CONV_WIDTH, w), lambda b, g, t: (0, g)),
                  pl.BlockSpec((CONV_WIDTH, w), lambda b, g, t: (0, g)),
                  pl.BlockSpec((1, 128), lambda b, g, t: (0, 0)),
                  pl.BlockSpec((1, 128), lambda b, g, t: (0, 0)),
                  pl.BlockSpec((1, 128), lambda b, g, t: (0, 0))],
        out_specs=pl.BlockSpec((rows, w), lambda b, g, t: (b * nt + t, g)),
        out_shape=jax.ShapeDtypeStruct((batch * seq, GDN_WIDTH), BF16),
        scratch_shapes=[pltpu.VMEM((3, rows + HALO, w), F32), pltpu.VMEM((hps, HEAD_DIM, HEAD_DIM), F32)],
        compiler_params=_cparams(("parallel", "parallel", "arbitrary"), 32),
        name="gated_deltanet",
    )(proj, proj, proj, proj, proj, cwq, cwk, cwv, pad(a_log), pad(dt_bias), gdn_norm_w.reshape(1, 128))


def _sb_kernel(q_ref, k_ref, v_ref, nw_ref, o_ref, *, tq, tk, hps):
    qi = pl.program_id(2)
    q0 = qi * tq
    hs = range(hps)
    sls = [slice(j * HEAD_DIM, (j + 1) * HEAD_DIM) for j in hs]
    q = [q_ref[:, sl].astype(BF16) for sl in sls]
    rowi = lax.broadcasted_iota(jnp.int32, (tq, tk), 0)
    coli = lax.broadcasted_iota(jnp.int32, (tq, tk), 1)
    jj = lax.broadcasted_iota(jnp.int32, (tk, tk), 0)
    ss = lax.broadcasted_iota(jnp.int32, (tk, tk), 1)
    later = (jj > ss).astype(BF16)
    nk = (qi + 1) * (tq // tk)

    def cond(carry):
        i, _, _, live = carry
        return jnp.logical_and(i < nk, live > SB_LOG_CUTOFF)

    def body(carry):
        i, acc, run, _ = carry
        k0 = pl.multiple_of((nk - 1 - i) * tk, tk)
        causal = (coli + k0) < (rowi + q0)
        kb = [k_ref[pl.ds(k0, tk), sl].astype(BF16) for sl in sls]
        vb = [v_ref[pl.ds(k0, tk), sl].astype(BF16) for sl in sls]
        z = [_dot_nt(q[j], kb[j]) * ATTN_SCALE for j in hs]
        sp = [_softplus(z[j]) for j in hs]
        log_not = [jnp.where(causal, -sp[j], 0.0) for j in hs]
        parts = [_split(log_not[j]) for j in hs]
        both = [_dot(jnp.concatenate(parts[j], axis=0), later) for j in hs]
        after = [both[j][:tq] + both[j][tq:] + run[j] for j in hs]
        a = [jnp.where(causal, jnp.exp((z[j] - sp[j]) + after[j]), 0.0).astype(BF16) for j in hs]
        acc = tuple(acc[j] + _dot(a[j], vb[j]) for j in hs)
        run = tuple(run[j] + jnp.sum(log_not[j], axis=1, keepdims=True) for j in hs)
        live = jnp.max(run[0])
        for j in range(1, hps):
            live = jnp.maximum(live, jnp.max(run[j]))
        return i + 1, acc, run, live

    init = (jnp.int32(0), tuple(jnp.zeros((tq, HEAD_DIM), F32) for _ in hs),
            tuple(jnp.zeros((tq, 1), F32) for _ in hs), jnp.float32(0.0))
    _, acc, _, _ = lax.while_loop(cond, body, init)
    for j in hs:
        o_ref[:, sls[j]] = _rms_rows(acc[j], nw_ref[:, sls[j]]).astype(o_ref.dtype)


def _sb(proj, sb_norm_w, batch, seq):
    tq = min(256, seq)
    tk = 128
    hps = 4
    w = hps * HEAD_DIM
    nq = seq // tq
    cq, ck, cv = COL_SQ // w, COL_SK // w, COL_SV // w
    return pl.pallas_call(
        functools.partial(_sb_kernel, tq=tq, tk=tk, hps=hps),
        grid=(batch, SB_HEADS // hps, nq),
        in_specs=[pl.BlockSpec((tq, w), lambda b, h, i: (b * nq + i, cq + h)),
                  pl.BlockSpec((seq, w), lambda b, h, i: (b, ck + h)),
                  pl.BlockSpec((seq, w), lambda b, h, i: (b, cv + h)),
                  pl.BlockSpec((1, w), lambda b, h, i: (0, h))],
        out_specs=pl.BlockSpec((tq, w), lambda b, h, i: (b * nq + i, h)),
        out_shape=jax.ShapeDtypeStruct((batch * seq, SB_WIDTH), BF16),
        compiler_params=_cparams(("parallel", "parallel", "arbitrary"), 48),
        name="stick_breaking",
    )(proj, proj, proj, sb_norm_w.reshape(1, SB_WIDTH))


def _mem_kernel(q_ref, kv_ref, nw_ref, o_ref):
    for h in range(MEM_HEADS):
        sl = slice(h * HEAD_DIM, (h + 1) * HEAD_DIM)
        q = q_ref[:, sl].astype(BF16)
        k = kv_ref[:, sl].astype(BF16)
        v = kv_ref[:, MEM_WIDTH + h * HEAD_DIM:MEM_WIDTH + (h + 1) * HEAD_DIM].astype(BF16)
        s = _dot_nt(q, k) * ATTN_SCALE
        e = jnp.exp(s - jnp.max(s, axis=-1, keepdims=True))
        p = e / jnp.sum(e, axis=-1, keepdims=True)
        o = _dot(p.astype(BF16), v)
        o_ref[:, sl] = _rms_rows(o, nw_ref[:, sl]).astype(o_ref.dtype)


def _mem_attn(proj, kv, mem_norm_w, batch, seq, n_mem):
    tq = min(512, seq)
    nq = seq // tq
    return pl.pallas_call(
        _mem_kernel,
        grid=(batch, nq),
        in_specs=[pl.BlockSpec((tq, MEM_WIDTH), lambda b, i: (b * nq + i, COL_MQ // MEM_WIDTH)),
                  pl.BlockSpec((n_mem, 2 * MEM_WIDTH), lambda b, i: (b, 0)),
                  pl.BlockSpec((1, MEM_WIDTH), lambda b, i: (0, 0))],
        out_specs=pl.BlockSpec((tq, MEM_WIDTH), lambda b, i: (b * nq + i, 0)),
        out_shape=jax.ShapeDtypeStruct((batch * seq, MEM_WIDTH), BF16),
        compiler_params=_cparams(("parallel", "parallel"), 32),
        name="memory_attention",
    )(proj, kv, mem_norm_w.reshape(1, MEM_WIDTH))


def _out_ln_kernel(og_ref, os_ref, om_ref, wg_ref, ws_ref, wm_ref, x_ref, ew_ref, eb_ref, lw_ref, lb_ref,
                   o_ref, ob_ref):
    mix = _dot(og_ref[...], wg_ref[...]) + _dot(os_ref[...], ws_ref[...]) + _dot(om_ref[...], wm_ref[...])
    res = _layer_norm_rows(x_ref[...], ew_ref[...], eb_ref[...])
    y = _layer_norm_rows(DEEPNORM_ALPHA * res + mix, lw_ref[...], lb_ref[...])
    o_ref[...] = y
    ob_ref[...] = y.astype(BF16)


def _out_ln(o_gdn, o_sb, o_mem, w_out, x2d, ew, eb, lw, lb):
    n, d = x2d.shape
    tm = min(256, n)
    wg, ws, wm = w_out[:GDN_WIDTH], w_out[GDN_WIDTH:GDN_WIDTH + SB_WIDTH], w_out[GDN_WIDTH + SB_WIDTH:]
    rowb = lambda width: pl.BlockSpec((tm, width), lambda i: (i, 0))
    full = lambda r, c: pl.BlockSpec((r, c), lambda i: (0, 0))
    return pl.pallas_call(
        _out_ln_kernel,
        grid=(n // tm,),
        in_specs=[rowb(GDN_WIDTH), rowb(SB_WIDTH), rowb(MEM_WIDTH),
                  full(GDN_WIDTH, d), full(SB_WIDTH, d), full(MEM_WIDTH, d),
                  rowb(d), full(1, d), full(1, d), full(1, d), full(1, d)],
        out_specs=[rowb(d), rowb(d)],
        out_shape=[jax.ShapeDtypeStruct((n, d), F32), jax.ShapeDtypeStruct((n, d), BF16)],
        compiler_params=_cparams(("parallel",), 60),
        name="out_proj_ln",
    )(o_gdn, o_sb, o_mem, wg, ws, wm, x2d, ew.reshape(1, d), eb.reshape(1, d), lw.reshape(1, d), lb.reshape(1, d))


def _peer_score_kernel(h_ref, wq_ref, k1_ref, k2_ref, s1_ref, s2_ref):
    q = _dot(h_ref[...], wq_ref[...]).astype(BF16)
    k1 = k1_ref[...]
    k2 = k2_ref[...]
    for h in range(PEER_HEADS):
        base = h * 2 * PEER_D_HALF
        s1_ref[h] = _dot_nt(k1, q[:, base:base + PEER_D_HALF])
        s2_ref[h] = _dot_nt(k2, q[:, base + PEER_D_HALF:base + 2 * PEER_D_HALF])


def _peer_scores(hb, wq, keys1, keys2):
    n, d = hb.shape
    tb = min(512, n)
    dq = wq.shape[1]
    out = jax.ShapeDtypeStruct((PEER_HEADS, PEER_N_KEYS, n), F32)
    ospec = pl.BlockSpec((PEER_HEADS, PEER_N_KEYS, tb), lambda i: (0, 0, i))
    return pl.pallas_call(
        _peer_score_kernel,
        grid=(n // tb,),
        in_specs=[pl.BlockSpec((tb, d), lambda i: (i, 0)),
                  pl.BlockSpec((d, dq), lambda i: (0, 0)),
                  pl.BlockSpec((PEER_N_KEYS, PEER_D_HALF), lambda i: (0, 0)),
                  pl.BlockSpec((PEER_N_KEYS, PEER_D_HALF), lambda i: (0, 0))],
        out_specs=[ospec, ospec],
        out_shape=[out, out],
        compiler_params=_cparams(("parallel",), 58),
        name="peer_scores",
    )(hb, wq, keys1, keys2)


NTOP = PEER_TOPK + 1
VROWS = 24


def _extract_top(cur, n, rowid, out_ref=None):
    vals = []
    nrows = cur.shape[0]
    for it in range(n):
        m = jnp.max(cur, axis=0, keepdims=True)
        first = jnp.min(jnp.where(cur == m, rowid, float(nrows)), axis=0, keepdims=True)
        cur = jnp.where(rowid == first, NEG_BIG, cur)
        vals.append(m)
        if out_ref is not None:
            out_ref[it:it + 1, :] = m
    return vals


def _extract_top_distinct(cur, n, out_ref):
    for it in range(n):
        m = jnp.max(cur, axis=0, keepdims=True)
        cur = jnp.where(cur == m, NEG_BIG, cur)
        out_ref[it:it + 1, :] = m
    return jnp.sum(jnp.where(cur == NEG_BIG, 1.0, 0.0), axis=0, keepdims=True)


def _peer_gate_kernel(s1_ref, s2_ref, t1_ref, d1_ref, e2_ref, v1_s, v2_s, top_s):
    s1 = s1_ref[0]
    s2 = s2_ref[0]
    tb = s1.shape[1]
    v1_s[...] = jnp.full(v1_s.shape, NEG_BIG, F32)
    v2_s[...] = jnp.full(v2_s.shape, NEG_BIG, F32)
    lost = jnp.maximum(_extract_top_distinct(s1, NTOP, v1_s), _extract_top_distinct(s2, NTOP, v2_s))

    @pl.when(jnp.max(lost) > NTOP)
    def _():
        rowid = lax.broadcasted_iota(jnp.int32, s1.shape, 0).astype(F32)
        _extract_top(s1, NTOP, rowid, v1_s)
        _extract_top(s2, NTOP, rowid, v2_s)

    v1 = [v1_s[a:a + 1, :] for a in range(8)]
    r8 = lax.broadcasted_iota(jnp.int32, (8, tb), 0)
    pieces = [v1[0] + v2_s[...]]
    pads = 2 * (VROWS - NTOP)
    for a in range(1, 8):
        nb = NTOP // (a + 1)
        pads += 8 - nb
        pieces.append(jnp.where(r8 < nb, v1[a] + v2_s[0:8, :], NEG_BIG))
    pieces.append(v1_s[8:VROWS, :] + v2_s[0:1, :])
    cand = jnp.concatenate(pieces, axis=0)
    lost = _extract_top_distinct(cand, NTOP, top_s)

    @pl.when(jnp.max(lost) > NTOP + pads)
    def _():
        candid = lax.broadcasted_iota(jnp.int32, cand.shape, 0).astype(F32)
        _extract_top(cand, NTOP, candid, top_s)

    top0 = top_s[0:1, :]
    thr = 0.5 * (top_s[PEER_TOPK - 1:PEER_TOPK, :] + top_s[PEER_TOPK:PEER_TOPK + 1, :])
    zsum = jnp.sum(jnp.where(cand >= thr, jnp.exp(cand - top0), 0.0), axis=0, keepdims=True)
    m2 = v2_s[0:1, :]
    t1_ref[0] = jnp.exp((thr - s1) - m2)
    d1_ref[0] = jnp.exp(s1 - v1[0]) / zsum
    e2_ref[0] = jnp.exp(s2 - m2)


def _peer_gates(s1t, s2t):
    nh, nk, n = s1t.shape
    tb = min(512, n)
    spec = pl.BlockSpec((1, nk, tb), lambda h, i: (h, 0, i))
    out = jax.ShapeDtypeStruct((nh, nk, n), F32)
    return pl.pallas_call(
        _peer_gate_kernel,
        grid=(nh, n // tb),
        in_specs=[spec, spec],
        out_specs=[spec, spec, spec],
        out_shape=[out, out, out],
        scratch_shapes=[pltpu.VMEM((VROWS, tb), F32)] * 3,
        compiler_params=_cparams(("parallel", "parallel"), 32),
        name="peer_gates",
    )(s1t, s2t)


SQRT_HALF = 0.7071067811865476
PEER_GROUP = 8
PEER_LANES = 512
PEER_TILE = 256


def _gelu(x):
    return 0.5 * x * (1.0 + lax.erf(x * SQRT_HALF))


def _peer_mix_kernel(x_ref, u_ref, vt_ref, t1_ref, d1_ref, e2_ref, o_ref, *scratch):
    nchunk = len(scratch) // 2
    act_refs = scratch[:nchunk]
    p_refs = scratch[nchunk:]
    g = pl.program_id(1)
    nk = PEER_N_KEYS

    def cols(c):
        return slice(c * PEER_LANES, (c + 1) * PEER_LANES)

    def scores(c):
        act_refs[c][...] = _dot_nt(u_ref[...], x_ref[cols(c), :])

    def weigh(c, i):
        rs = slice(i * nk, (i + 1) * nk)
        for t0 in range(0, PEER_LANES, PEER_TILE):
            ls = slice(t0, t0 + PEER_TILE)
            cs = slice(c * PEER_LANES + t0, c * PEER_LANES + t0 + PEER_TILE)
            w = jnp.zeros((nk, PEER_TILE), F32)
            for h in range(PEER_HEADS):
                e2 = e2_ref[h, :, cs]
                w = w + jnp.where(e2 >= t1_ref[h, i:i + 1, cs], e2, 0.0) * d1_ref[h, i:i + 1, cs]
            p_refs[c][rs, ls] = (_gelu(act_refs[c][rs, ls]) * w).astype(BF16)

    def mix(c, pair, assign):
        rs = slice(pair * 2 * nk, (pair + 1) * 2 * nk)
        upd = _dot(vt_ref[:, rs], p_refs[c][rs, :])
        if assign:
            o_ref[:, cols(c)] = upd
        else:
            o_ref[:, cols(c)] += upd

    def step(first_group):
        scores(0)
        for c in range(nchunk):
            if c + 1 < nchunk:
                scores(c + 1)
            for pair in range(PEER_GROUP // 2):
                weigh(c, 2 * pair)
                weigh(c, 2 * pair + 1)
                if pair >= 1:
                    mix(c, pair - 1, first_group and pair == 1)
            mix(c, PEER_GROUP // 2 - 1, False)

    @pl.when(g == 0)
    def _():
        step(True)

    @pl.when(g != 0)
    def _():
        step(False)


def _peer_mix(hb, u_b, vt_b, t1t, d1t, e2t):
    n, d = hb.shape
    tb = min(1024, n)
    ne = u_b.shape[0]
    eb = PEER_GROUP * PEER_N_KEYS
    ng = ne // eb
    gspec = pl.BlockSpec((PEER_HEADS, PEER_GROUP, tb), lambda t, g: (0, g, t))
    kspec = pl.BlockSpec((PEER_HEADS, PEER_N_KEYS, tb), lambda t, g: (0, 0, t))
    nchunk = tb // PEER_LANES
    return pl.pallas_call(
        _peer_mix_kernel,
        grid=(n // tb, ng),
        in_specs=[pl.BlockSpec((tb, d), lambda t, g: (t, 0)),
                  pl.BlockSpec((eb, d), lambda t, g: (g, 0)),
                  pl.BlockSpec((d, eb), lambda t, g: (0, g)),
                  gspec, gspec, kspec],
        out_specs=pl.BlockSpec((d, tb), lambda t, g: (0, t)),
        out_shape=jax.ShapeDtypeStruct((d, n), F32),
        scratch_shapes=([pltpu.VMEM((eb, PEER_LANES), F32)] * nchunk
                        + [pltpu.VMEM((eb, PEER_LANES), BF16)] * nchunk),
        compiler_params=_cparams(("parallel", "arbitrary"), 60),
        name="peer_mix",
    )(hb, u_b, vt_b, t1t, d1t, e2t)


def _final_ln_kernel(h_ref, ft_ref, w_ref, b_ref, o_ref):
    y = DEEPNORM_ALPHA * h_ref[...] + ft_ref[...].T
    o_ref[...] = _layer_norm_rows(y, w_ref[...], b_ref[...])


def _final_ln(h, ffn_t, w, b):
    n, d = h.shape
    tm = min(512, n)
    return pl.pallas_call(
        _final_ln_kernel,
        grid=(n // tm,),
        in_specs=[pl.BlockSpec((tm, d), lambda i: (i, 0)),
                  pl.BlockSpec((d, tm), lambda i: (0, i)),
                  pl.BlockSpec((1, d), lambda i: (0, 0)),
                  pl.BlockSpec((1, d), lambda i: (0, 0))],
        out_specs=pl.BlockSpec((tm, d), lambda i: (i, 0)),
        out_shape=jax.ShapeDtypeStruct((n, d), F32),
        compiler_params=_cparams(("parallel",), 40),
        name="final_ln",
    )(h, ffn_t, w.reshape(1, d), b.reshape(1, d))


def _reorder_w_in(w_in):
    o_ab = 4 * GDN_WIDTH
    d = w_in.shape[0]
    main = jnp.concatenate([w_in[:, :o_ab], w_in[:, o_ab + 2 * GDN_HEADS:]], axis=1)
    ab = jnp.zeros((d, PROJ_COLS - COL_AB), w_in.dtype).at[:, :2 * GDN_HEADS].set(w_in[:, o_ab:o_ab + 2 * GDN_HEADS])
    return jnp.concatenate([main, ab], axis=1)


def kernel(x, mem, ln_emb_w, ln_emb_b, w_in, conv_w, a_log, dt_bias, gdn_norm_w, sb_norm_w,
           mem_norm_w, ln_mem_w, ln_mem_b, w_mem_kv, w_out, ln1_w, ln1_b, peer_wq, peer_keys1,
           peer_keys2, peer_u, peer_v, ln2_w, ln2_b):
    batch, seq, d = x.shape
    n_mem = mem.shape[1]
    l = 0
    x2d = x.reshape(batch * seq, d)
    proj = _ln_mm(x2d, ln_emb_w, ln_emb_b, _reorder_w_in(w_in[l].astype(BF16)), PROJ_TILE)

    o_gdn = _gdn(proj, conv_w[l], a_log[l], dt_bias[l], gdn_norm_w[l], batch, seq)
    o_sb = _sb(proj, sb_norm_w[l], batch, seq)
    _, mb = _ln(mem.reshape(batch * n_mem, d), ln_mem_w[l], ln_mem_b[l])
    kv = _mm(mb, w_mem_kv[l].astype(BF16), 2 * MEM_WIDTH // 2, "mem_kv")
    o_mem = _mem_attn(proj, kv, mem_norm_w[l], batch, seq, n_mem)

    h1, h1b = _out_ln(o_gdn, o_sb, o_mem, w_out[l].astype(BF16), x2d, ln_emb_w, ln_emb_b, ln1_w[l], ln1_b[l])

    s1t, s2t = _peer_scores(h1b, peer_wq[l].astype(BF16), peer_keys1[l].astype(BF16), peer_keys2[l].astype(BF16))
    t1t, d1t, e2t = _peer_gates(s1t, s2t)
    ffn_t = _peer_mix(h1b, peer_u[l].astype(BF16), peer_v[l].astype(BF16).T, t1t, d1t, e2t)
    out = _final_ln(h1, ffn_t, ln2_w[l], ln2_b[l])
    return out.reshape(batch, seq, d)
```

```python
import functools

import jax
import jax.numpy as jnp
from jax import lax
from jax.experimental import pallas as pl
from jax.experimental.pallas import tpu as pltpu

F32 = jnp.float32
BF16 = jnp.bfloat16

HEAD_DIM = 128
GDN_HEADS = 8
SB_HEADS = 4
MEM_HEADS = 4
GDN_WIDTH = GDN_HEADS * HEAD_DIM
SB_WIDTH = SB_HEADS * HEAD_DIM
MEM_WIDTH = MEM_HEADS * HEAD_DIM
CONV_WIDTH = 4
PEER_HEADS = 8
PEER_N_KEYS = 128
PEER_TOPK = 16
PEER_D_HALF = 128
DEPTH = 1
DEEPNORM_ALPHA = (2 * DEPTH) ** 0.25
LN_EPS = 1e-5
RMS_EPS = 1e-6
ATTN_SCALE = HEAD_DIM ** -0.5

COL_GQ = 0
COL_GK = GDN_WIDTH
COL_GV = 2 * GDN_WIDTH
COL_GZ = 3 * GDN_WIDTH
COL_SQ = 4 * GDN_WIDTH
COL_SK = COL_SQ + SB_WIDTH
COL_SV = COL_SK + SB_WIDTH
COL_MQ = COL_SV + SB_WIDTH
COL_AB = COL_MQ + MEM_WIDTH
MXU_WIDTH = 256
PROJ_TILE = 5 * MXU_WIDTH
PROJ_COLS = -(-(COL_AB + 128) // PROJ_TILE) * PROJ_TILE

GDN_CHUNK = 128
HALO = 8
NEG_BIG = -3.0e38
SB_LOG_CUTOFF = -104.0
MIB = 1024 * 1024


def _cparams(sem, vmem_mib):
    return pltpu.CompilerParams(dimension_semantics=sem, vmem_limit_bytes=vmem_mib * MIB)


def _dot(a, b):
    return jnp.dot(a, b, preferred_element_type=F32)


def _dot_nt(a, b):
    return lax.dot_general(a, b, (((1,), (1,)), ((), ())), preferred_element_type=F32)


def _split(a):
    hi = a.astype(BF16)
    lo = (a - hi.astype(F32)).astype(BF16)
    return hi, lo


def _dot3(a, b):
    return _dot3_many([a], [b])[0]


def _dot3_many(as_, bs_):
    sa = [_split(a) for a in as_]
    sb = [_split(b) for b in bs_]
    both = [_dot(jnp.concatenate(x, axis=0), y[0]) for x, y in zip(sa, sb)]
    hl = [_dot(x[0], y[1]) for x, y in zip(sa, sb)]
    return [t[:a.shape[0]] + (t[a.shape[0]:] + m) for a, t, m in zip(as_, both, hl)]


def _softplus(x):
    return jnp.maximum(x, 0.0) + jnp.log(1.0 + jnp.exp(-jnp.abs(x)))


def _sigmoid(x):
    return 1.0 / (1.0 + jnp.exp(-x))


def _silu(x):
    return x * _sigmoid(x)


def _layer_norm_rows(x, w, b):
    mu = jnp.mean(x, axis=-1, keepdims=True)
    xc = x - mu
    var = jnp.mean(xc * xc, axis=-1, keepdims=True)
    return xc * lax.rsqrt(var + LN_EPS) * w + b


def _rms_rows(x, w):
    return x * lax.rsqrt(jnp.mean(x * x, axis=-1, keepdims=True) + RMS_EPS) * w


def _ln_kernel(x_ref, w_ref, b_ref, o_ref, ob_ref):
    y = _layer_norm_rows(x_ref[...], w_ref[...], b_ref[...])
    o_ref[...] = y
    ob_ref[...] = y.astype(BF16)


def _ln(x2d, w, b):
    n, d = x2d.shape
    tm = min(512, n)
    return pl.pallas_call(
        _ln_kernel,
        grid=(n // tm,),
        in_specs=[pl.BlockSpec((tm, d), lambda i: (i, 0)),
                  pl.BlockSpec((1, d), lambda i: (0, 0)),
                  pl.BlockSpec((1, d), lambda i: (0, 0))],
        out_specs=[pl.BlockSpec((tm, d), lambda i: (i, 0)),
                   pl.BlockSpec((tm, d), lambda i: (i, 0))],
        out_shape=[jax.ShapeDtypeStruct((n, d), F32), jax.ShapeDtypeStruct((n, d), BF16)],
        compiler_params=_cparams(("parallel",), 32),
        name="layer_norm",
    )(x2d, w.reshape(1, d), b.reshape(1, d))


def _mm_kernel(a_ref, w_ref, o_ref):
    o_ref[...] = _dot(a_ref[...], w_ref[...]).astype(o_ref.dtype)


def _mm(a, w, tn, name):
    m, k = a.shape
    n = w.shape[1]
    tm = min(1024, m)
    return pl.pallas_call(
        _mm_kernel,
        grid=(m // tm, n // tn),
        in_specs=[pl.BlockSpec((tm, k), lambda i, j: (i, 0)),
                  pl.BlockSpec((k, tn), lambda i, j: (0, j))],
        out_specs=pl.BlockSpec((tm, tn), lambda i, j: (i, j)),
        out_shape=jax.ShapeDtypeStruct((m, n), F32),
        compiler_params=_cparams(("parallel", "arbitrary"), 40),
        name=name,
    )(a, w)


def _ln_mm_kernel(x_ref, lw_ref, lb_ref, w_ref, o_ref, hb_s):
    @pl.when(pl.program_id(1) == 0)
    def _():
        hb_s[...] = _layer_norm_rows(x_ref[...], lw_ref[...], lb_ref[...]).astype(BF16)

    o_ref[...] = _dot(hb_s[...], w_ref[...])


def _ln_mm(x2d, lw, lb, w, tn):
    m, k = x2d.shape
    n = w.shape[1]
    tm = min(1024, m)
    return pl.pallas_call(
        _ln_mm_kernel,
        grid=(m // tm, n // tn),
        in_specs=[pl.BlockSpec((tm, k), lambda i, j: (i, 0)),
                  pl.BlockSpec((1, k), lambda i, j: (0, 0)),
                  pl.BlockSpec((1, k), lambda i, j: (0, 0)),
                  pl.BlockSpec((k, tn), lambda i, j: (0, j))],
        out_specs=pl.BlockSpec((tm, tn), lambda i, j: (i, j)),
        out_shape=jax.ShapeDtypeStruct((m, n), F32),
        scratch_shapes=[pltpu.VMEM((tm, k), BF16)],
        compiler_params=_cparams(("parallel", "arbitrary"), 56),
        name="ln_in_proj",
    )(x2d, lw.reshape(1, k), lb.reshape(1, k), w)


def _gdn_kernel(q_ref, k_ref, v_ref, z_ref, ab_ref, cwq_ref, cwk_ref, cwv_ref, alog_ref, dtb_ref, nw_ref,
                o_ref, xbuf, s_ref, *, hps, rows):
    C = GDN_CHUNK
    hg = pl.program_id(1)
    tt = pl.program_id(2)

    @pl.when(tt == 0)
    def _():
        xbuf[:, 0:HALO, :] = jnp.zeros((3, HALO, hps * HEAD_DIM), F32)
        s_ref[...] = jnp.zeros_like(s_ref)

    xbuf[0, HALO:HALO + rows, :] = q_ref[...]
    xbuf[1, HALO:HALO + rows, :] = k_ref[...]
    xbuf[2, HALO:HALO + rows, :] = v_ref[...]

    row = lax.broadcasted_iota(jnp.int32, (C, C), 0)
    col = lax.broadcasted_iota(jnp.int32, (C, C), 1)
    tril = row >= col
    strict = row > col
    eye = (row == col).astype(F32)
    tril_f = tril.astype(F32)
    diag16 = strict & ((row // 16) == (col // 16))
    levels = [strict & ((row // (2 * s)) == (col // (2 * s))) & ((row // s) != (col // s)) for s in (16, 32, 64)]
    lane = lax.broadcasted_iota(jnp.int32, (C, HEAD_DIM), 1)
    neg_a = -jnp.exp(alog_ref[...])
    dtb = dtb_ref[...]
    nw = nw_ref[...]
    cws = (cwq_ref[...], cwk_ref[...], cwv_ref[...])

    def conv(which, r0, sl):
        win = xbuf[which, pl.ds(r0, C + HALO), sl]
        cw = cws[which][:, sl]
        y = win[HALO:] * cw[CONV_WIDTH - 1:CONV_WIDTH, :]
        for back in range(1, CONV_WIDTH):
            shifted = pltpu.roll(win, back, 0)[HALO:]
            y = y + shifted * cw[CONV_WIDTH - 1 - back:CONV_WIDTH - back, :]
        return _silu(y)

    def chunk(c, carry):
        r0 = pl.multiple_of(c * C, C)
        ab = ab_ref[pl.ds(r0, C), :]
        g_all = neg_a * _softplus(ab + dtb)
        beta_all = _sigmoid(ab)
        gcum_all = _dot3(tril_f, g_all)
        gcum_t = gcum_all.T
        hs = range(hps)
        sls = [slice(j * HEAD_DIM, (j + 1) * HEAD_DIM) for j in hs]
        heads = [hg * hps + j for j in hs]
        q = [conv(0, r0, sl) for sl in sls]
        k = [conv(1, r0, sl) for sl in sls]
        v = [conv(2, r0, sl) for sl in sls]
        q = [x * lax.rsqrt(jnp.sum(x * x, axis=-1, keepdims=True) + RMS_EPS) * ATTN_SCALE for x in q]
        k = [x * lax.rsqrt(jnp.sum(x * x, axis=-1, keepdims=True) + RMS_EPS) for x in k]
        gcol = [jnp.sum(jnp.where(lane == h, gcum_all, 0.0), axis=1, keepdims=True) for h in heads]
        grow = [jnp.sum(jnp.where(row == h, gcum_t, 0.0), axis=0, keepdims=True) for h in heads]
        beta = [jnp.sum(jnp.where(lane == GDN_HEADS + h, beta_all, 0.0), axis=1, keepdims=True) for h in heads]
        decay = [jnp.where(tril, jnp.exp(jnp.where(tril, gcol[j] - grow[j], 0.0)), 0.0) for j in hs]
        kb = [k[j] * beta[j] for j in hs]
        vb = [v[j] * beta[j] for j in hs]
        kk = _dot3_many(kb, [x.T for x in k])
        low = [jnp.where(strict, kk[j] * decay[j], 0.0) for j in hs]
        a = [jnp.where(diag16, -low[j], 0.0) for j in hs]
        inv = [eye + a[j] for j in hs]
        a = _dot3_many(a, a)
        for _ in range(2):
            both = _dot3_many([jnp.concatenate([inv[j], a[j]], axis=0) for j in hs], a)
            inv = [inv[j] + both[j][:C] for j in hs]
            a = [both[j][C:] for j in hs]
        t = _dot3_many(inv, a)
        inv = [inv[j] + t[j] for j in hs]
        for lv in levels:
            t = _dot3_many([jnp.where(lv, low[j], 0.0) for j in hs], inv)
            t = _dot3_many(inv, t)
            inv = [inv[j] - t[j] for j in hs]
        eg = [jnp.exp(gcol[j]) for j in hs]
        uw = _dot3_many(inv, [jnp.concatenate([vb[j], kb[j] * eg[j]], axis=1) for j in hs])
        qk = [_dot_nt(q[j].astype(BF16), k[j].astype(BF16)) for j in hs]
        qk = [jnp.where(tril, qk[j] * decay[j], 0.0).astype(BF16) for j in hs]
        s = [s_ref[j] for j in hs]
        sb = [x.astype(BF16) for x in s]
        ws = [_dot(uw[j][:, HEAD_DIM:].astype(BF16), sb[j]) for j in hs]
        o1 = [_dot((q[j] * eg[j]).astype(BF16), sb[j]) for j in hs]
        vnb = [(uw[j][:, :HEAD_DIM] - ws[j]).astype(BF16) for j in hs]
        o2 = [_dot(qk[j], vnb[j]) for j in hs]
        g_last = [gcol[j][C - 1:C, :] for j in hs]
        kdt = [(k[j] * jnp.exp(g_last[j] - gcol[j])).T.astype(BF16) for j in hs]
        upd = [_dot(kdt[j], vnb[j]) for j in hs]
        for j in hs:
            s_ref[j] = s[j] * jnp.exp(g_last[j]) + upd[j]
            z = z_ref[pl.ds(r0, C), sls[j]]
            o_ref[pl.ds(r0, C), sls[j]] = (_rms_rows(o1[j] + o2[j], nw) * _silu(z)).astype(o_ref.dtype)
        return carry

    lax.fori_loop(0, rows // C, chunk, 0)
    xbuf[:, 0:HALO, :] = xbuf[:, rows:rows + HALO, :]


def _gdn(proj, conv_w, a_log, dt_bias, gdn_norm_w, batch, seq):
    hps = 8
    rows = min(512, seq)
    nt = seq // rows
    w = hps * HEAD_DIM
    cwq, cwk, cwv = (conv_w[:, i * GDN_WIDTH:(i + 1) * GDN_WIDTH] for i in range(3))
    pad = lambda t: jnp.zeros((1, 128), F32).at[0, :GDN_HEADS].set(t)

    def col(base):
        return lambda b, g, t: (b * nt + t, base // w + g)

    return pl.pallas_call(
        functools.partial(_gdn_kernel, hps=hps, rows=rows),
        grid=(batch, GDN_HEADS // hps, nt),
        in_specs=[pl.BlockSpec((rows, w), col(COL_GQ)),
                  pl.BlockSpec((rows, w), col(COL_GK)),
                  pl.BlockSpec((rows, w), col(COL_GV)),
                  pl.BlockSpec((rows, w), col(COL_GZ)),
                  pl.BlockSpec((rows, 128), lambda b, g, t: (b * nt + t, COL_AB // 128)),
                  pl.BlockSpec((CONV_WIDTH, w), lambda b, g, t: (0, g)),
                  pl.BlockSpec((CONV_WIDTH, w), lambda b, g, t: (0, g)),
                  pl.BlockSpec((CONV_WIDTH, w), lambda b, g, t: (0, g)),
                  pl.BlockSpec((1, 128), lambda b, g, t: (0, 0)),
                  pl.BlockSpec((1, 128), lambda b, g, t: (0, 0)),
                  pl.BlockSpec((1, 128), lambda b, g, t: (0, 0))],
        out_specs=pl.BlockSpec((rows, w), lambda b, g, t: (b * nt + t, g)),
        out_shape=jax.ShapeDtypeStruct((batch * seq, GDN_WIDTH), BF16),
        scratch_shapes=[pltpu.VMEM((3, rows + HALO, w), F32), pltpu.VMEM((hps, HEAD_DIM, HEAD_DIM), F32)],
        compiler_params=_cparams(("parallel", "parallel", "arbitrary"), 32),
        name="gated_deltanet",
    )(proj, proj, proj, proj, proj, cwq, cwk, cwv, pad(a_log), pad(dt_bias), gdn_norm_w.reshape(1, 128))


def _sb_kernel(q_ref, k_ref, v_ref, nw_ref, o_ref, *, tq, tk, hps):
    qi = pl.program_id(2)
    q0 = qi * tq
    hs = range(hps)
    sls = [slice(j * HEAD_DIM, (j + 1) * HEAD_DIM) for j in hs]
    q = [q_ref[:, sl].astype(BF16) for sl in sls]
    rowi = lax.broadcasted_iota(jnp.int32, (tq, tk), 0)
    coli = lax.broadcasted_iota(jnp.int32, (tq, tk), 1)
    jj = lax.broadcasted_iota(jnp.int32, (tk, tk), 0)
    ss = lax.broadcasted_iota(jnp.int32, (tk, tk), 1)
    later = (jj > ss).astype(BF16)
    nk = (qi + 1) * (tq // tk)

    def cond(carry):
        i, _, _, live = carry
        return jnp.logical_and(i < nk, live > SB_LOG_CUTOFF)

    def body(carry):
        i, acc, run, _ = carry
        k0 = pl.multiple_of((nk - 1 - i) * tk, tk)
        causal = (coli + k0) < (rowi + q0)
        kb = [k_ref[pl.ds(k0, tk), sl].astype(BF16) for sl in sls]
        vb = [v_ref[pl.ds(k0, tk), sl].astype(BF16) for sl in sls]
        z = [_dot_nt(q[j], kb[j]) * ATTN_SCALE for j in hs]
        sp = [_softplus(z[j]) for j in hs]
        log_not = [jnp.where(causal, -sp[j], 0.0) for j in hs]
        parts = [_split(log_not[j]) for j in hs]
        both = [_dot(jnp.concatenate(parts[j], axis=0), later) for j in hs]
        after = [both[j][:tq] + both[j][tq:] + run[j] for j in hs]
        a = [jnp.where(causal, jnp.exp((z[j] - sp[j]) + after[j]), 0.0).astype(BF16) for j in hs]
        acc = tuple(acc[j] + _dot(a[j], vb[j]) for j in hs)
        run = tuple(run[j] + jnp.sum(log_not[j], axis=1, keepdims=True) for j in hs)
        live = jnp.max(run[0])
        for j in range(1, hps):
            live = jnp.maximum(live, jnp.max(run[j]))
        return i + 1, acc, run, live

    init = (jnp.int32(0), tuple(jnp.zeros((tq, HEAD_DIM), F32) for _ in hs),
            tuple(jnp.zeros((tq, 1), F32) for _ in hs), jnp.float32(0.0))
    _, acc, _, _ = lax.while_loop(cond, body, init)
    for j in hs:
        o_ref[:, sls[j]] = _rms_rows(acc[j], nw_ref[:, sls[j]]).astype(o_ref.dtype)


def _sb(proj, sb_norm_w, batch, seq):
    tq = min(256, seq)
    tk = 128
    hps = 4
    w = hps * HEAD_DIM
    nq = seq // tq
    cq, ck, cv = COL_SQ // w, COL_SK // w, COL_SV // w
    return pl.pallas_call(
        functools.partial(_sb_kernel, tq=tq, tk=tk, hps=hps),
        grid=(batch, SB_HEADS // hps, nq),
        in_specs=[pl.BlockSpec((tq, w), lambda b, h, i: (b * nq + i, cq + h)),
                  pl.BlockSpec((seq, w), lambda b, h, i: (b, ck + h)),
                  pl.BlockSpec((seq, w), lambda b, h, i: (b, cv + h)),
                  pl.BlockSpec((1, w), lambda b, h, i: (0, h))],
        out_specs=pl.BlockSpec((tq, w), lambda b, h, i: (b * nq + i, h)),
        out_shape=jax.ShapeDtypeStruct((batch * seq, SB_WIDTH), BF16),
        compiler_params=_cparams(("parallel", "parallel", "arbitrary"), 48),
        name="stick_breaking",
    )(proj, proj, proj, sb_norm_w.reshape(1, SB_WIDTH))


def _mem_kernel(q_ref, kv_ref, nw_ref, o_ref):
    for h in range(MEM_HEADS):
        sl = slice(h * HEAD_DIM, (h + 1) * HEAD_DIM)
        q = q_ref[:, sl].astype(BF16)
        k = kv_ref[:, sl].astype(BF16)
        v = kv_ref[:, MEM_WIDTH + h * HEAD_DIM:MEM_WIDTH + (h + 1) * HEAD_DIM].astype(BF16)
        s = _dot_nt(q, k) * ATTN_SCALE
        e = jnp.exp(s - jnp.max(s, axis=-1, keepdims=True))
        p = e / jnp.sum(e, axis=-1, keepdims=True)
        o = _dot(p.astype(BF16), v)
        o_ref[:, sl] = _rms_rows(o, nw_ref[:, sl]).astype(o_ref.dtype)


def _mem_attn(proj, kv, mem_norm_w, batch, seq, n_mem):
    tq = min(512, seq)
    nq = seq // tq
    return pl.pallas_call(
        _mem_kernel,
        grid=(batch, nq),
        in_specs=[pl.BlockSpec((tq, MEM_WIDTH), lambda b, i: (b * nq + i, COL_MQ // MEM_WIDTH)),
                  pl.BlockSpec((n_mem, 2 * MEM_WIDTH), lambda b, i: (b, 0)),
                  pl.BlockSpec((1, MEM_WIDTH), lambda b, i: (0, 0))],
        out_specs=pl.BlockSpec((tq, MEM_WIDTH), lambda b, i: (b * nq + i, 0)),
        out_shape=jax.ShapeDtypeStruct((batch * seq, MEM_WIDTH), BF16),
        compiler_params=_cparams(("parallel", "parallel"), 32),
        name="memory_attention",
    )(proj, kv, mem_norm_w.reshape(1, MEM_WIDTH))


def _out_ln_kernel(og_ref, os_ref, om_ref, wg_ref, ws_ref, wm_ref, x_ref, ew_ref, eb_ref, lw_ref, lb_ref,
                   o_ref, ob_ref):
    mix = _dot(og_ref[...], wg_ref[...]) + _dot(os_ref[...], ws_ref[...]) + _dot(om_ref[...], wm_ref[...])
    res = _layer_norm_rows(x_ref[...], ew_ref[...], eb_ref[...])
    y = _layer_norm_rows(DEEPNORM_ALPHA * res + mix, lw_ref[...], lb_ref[...])
    o_ref[...] = y
    ob_ref[...] = y.astype(BF16)


def _out_ln(o_gdn, o_sb, o_mem, w_out, x2d, ew, eb, lw, lb):
    n, d = x2d.shape
    tm = min(256, n)
    wg, ws, wm = w_out[:GDN_WIDTH], w_out[GDN_WIDTH:GDN_WIDTH + SB_WIDTH], w_out[GDN_WIDTH + SB_WIDTH:]
    rowb = lambda width: pl.BlockSpec((tm, width), lambda i: (i, 0))
    full = lambda r, c: pl.BlockSpec((r, c), lambda i: (0, 0))
    return pl.pallas_call(
        _out_ln_kernel,
        grid=(n // tm,),
        in_specs=[rowb(GDN_WIDTH), rowb(SB_WIDTH), rowb(MEM_WIDTH),
                  full(GDN_WIDTH, d), full(SB_WIDTH, d), full(MEM_WIDTH, d),
                  rowb(d), full(1, d), full(1, d), full(1, d), full(1, d)],
        out_specs=[rowb(d), rowb(d)],
        out_shape=[jax.ShapeDtypeStruct((n, d), F32), jax.ShapeDtypeStruct((n, d), BF16)],
        compiler_params=_cparams(("parallel",), 60),
        name="out_proj_ln",
    )(o_gdn, o_sb, o_mem, wg, ws, wm, x2d, ew.reshape(1, d), eb.reshape(1, d), lw.reshape(1, d), lb.reshape(1, d))


def _peer_score_kernel(h_ref, wq_ref, k1_ref, k2_ref, s1_ref, s2_ref):
    q = _dot(h_ref[...], wq_ref[...]).astype(BF16)
    k1 = k1_ref[...]
    k2 = k2_ref[...]
    for h in range(PEER_HEADS):
        base = h * 2 * PEER_D_HALF
        s1_ref[h] = _dot_nt(k1, q[:, base:base + PEER_D_HALF])
        s2_ref[h] = _dot_nt(k2, q[:, base + PEER_D_HALF:base + 2 * PEER_D_HALF])


def _peer_scores(hb, wq, keys1, keys2):
    n, d = hb.shape
    tb = min(512, n)
    dq = wq.shape[1]
    out = jax.ShapeDtypeStruct((PEER_HEADS, PEER_N_KEYS, n), F32)
    ospec = pl.BlockSpec((PEER_HEADS, PEER_N_KEYS, tb), lambda i: (0, 0, i))
    return pl.pallas_call(
        _peer_score_kernel,
        grid=(n // tb,),
        in_specs=[pl.BlockSpec((tb, d), lambda i: (i, 0)),
                  pl.BlockSpec((d, dq), lambda i: (0, 0)),
                  pl.BlockSpec((PEER_N_KEYS, PEER_D_HALF), lambda i: (0, 0)),
                  pl.BlockSpec((PEER_N_KEYS, PEER_D_HALF), lambda i: (0, 0))],
        out_specs=[ospec, ospec],
        out_shape=[out, out],
        compiler_params=_cparams(("parallel",), 58),
        name="peer_scores",
    )(hb, wq, keys1, keys2)


NTOP = PEER_TOPK + 1
VROWS = 24


def _extract_top(cur, n, rowid, out_ref=None):
    vals = []
    nrows = cur.shape[0]
    for it in range(n):
        m = jnp.max(cur, axis=0, keepdims=True)
        first = jnp.min(jnp.where(cur == m, rowid, float(nrows)), axis=0, keepdims=True)
        cur = jnp.where(rowid == first, NEG_BIG, cur)
        vals.append(m)
        if out_ref is not None:
            out_ref[it:it + 1, :] = m
    return vals


def _extract_top_distinct(cur, n, out_ref):
    for it in range(n):
        m = jnp.max(cur, axis=0, keepdims=True)
        cur = jnp.where(cur == m, NEG_BIG, cur)
        out_ref[it:it + 1, :] = m
    return jnp.sum(jnp.where(cur == NEG_BIG, 1.0, 0.0), axis=0, keepdims=True)


def _peer_gate_kernel(s1_ref, s2_ref, t1_ref, d1_ref, e2_ref, v1_s, v2_s, top_s):
    s1 = s1_ref[0]
    s2 = s2_ref[0]
    tb = s1.shape[1]
    v1_s[...] = jnp.full(v1_s.shape, NEG_BIG, F32)
    v2_s[...] = jnp.full(v2_s.shape, NEG_BIG, F32)
    lost = jnp.maximum(_extract_top_distinct(s1, NTOP, v1_s), _extract_top_distinct(s2, NTOP, v2_s))

    @pl.when(jnp.max(lost) > NTOP)
    def _():
        rowid = lax.broadcasted_iota(jnp.int32, s1.shape, 0).astype(F32)
        _extract_top(s1, NTOP, rowid, v1_s)
        _extract_top(s2, NTOP, rowid, v2_s)

    v1 = [v1_s[a:a + 1, :] for a in range(8)]
    r8 = lax.broadcasted_iota(jnp.int32, (8, tb), 0)
    pieces = [v1[0] + v2_s[...]]
    pads = 2 * (VROWS - NTOP)
    for a in range(1, 8):
        nb = NTOP // (a + 1)
        pads += 8 - nb
        pieces.append(jnp.where(r8 < nb, v1[a] + v2_s[0:8, :], NEG_BIG))
    pieces.append(v1_s[8:VROWS, :] + v2_s[0:1, :])
    cand = jnp.concatenate(pieces, axis=0)
    lost = _extract_top_distinct(cand, NTOP, top_s)

    @pl.when(jnp.max(lost) > NTOP + pads)
    def _():
        candid = lax.broadcasted_iota(jnp.int32, cand.shape, 0).astype(F32)
        _extract_top(cand, NTOP, candid, top_s)

    top0 = top_s[0:1, :]
    thr = 0.5 * (top_s[PEER_TOPK - 1:PEER_TOPK, :] + top_s[PEER_TOPK:PEER_TOPK + 1, :])
    zsum = jnp.sum(jnp.where(cand >= thr, jnp.exp(cand - top0), 0.0), axis=0, keepdims=True)
    m2 = v2_s[0:1, :]
    t1_ref[0] = jnp.exp((thr - s1) - m2)
    d1_ref[0] = jnp.exp(s1 - v1[0]) / zsum
    e2_ref[0] = jnp.exp(s2 - m2)


def _peer_gates(s1t, s2t):
    nh, nk, n = s1t.shape
    tb = min(1024, n)
    spec = pl.BlockSpec((1, nk, tb), lambda h, i: (h, 0, i))
    out = jax.ShapeDtypeStruct((nh, nk, n), F32)
    return pl.pallas_call(
        _peer_gate_kernel,
        grid=(nh, n // tb),
        in_specs=[spec, spec],
        out_specs=[spec, spec, spec],
        out_shape=[out, out, out],
        scratch_shapes=[pltpu.VMEM((VROWS, tb), F32)] * 3,
        compiler_params=_cparams(("parallel", "parallel"), 32),
        name="peer_gates",
    )(s1t, s2t)


SQRT_HALF = 0.7071067811865476
PEER_GROUP = 8
PEER_LANES = 512
PEER_TILE = 256


def _gelu(x):
    return 0.5 * x * (1.0 + lax.erf(x * SQRT_HALF))


def _peer_mix_kernel(x_ref, u_ref, vt_ref, t1_ref, d1_ref, e2_ref, o_ref, *scratch):
    nchunk = len(scratch) // 2
    act_refs = scratch[:nchunk]
    p_refs = scratch[nchunk:]
    g = pl.program_id(1)
    nk = PEER_N_KEYS

    def cols(c):
        return slice(c * PEER_LANES, (c + 1) * PEER_LANES)

    def scores(c):
        act_refs[c][...] = _dot_nt(u_ref[...], x_ref[cols(c), :])

    def weigh(c, i):
        rs = slice(i * nk, (i + 1) * nk)
        for t0 in range(0, PEER_LANES, PEER_TILE):
            ls = slice(t0, t0 + PEER_TILE)
            cs = slice(c * PEER_LANES + t0, c * PEER_LANES + t0 + PEER_TILE)
            w = jnp.zeros((nk, PEER_TILE), F32)
            for h in range(PEER_HEADS):
                e2 = e2_ref[h, :, cs]
                w = w + jnp.where(e2 >= t1_ref[h, i:i + 1, cs], e2, 0.0) * d1_ref[h, i:i + 1, cs]
            p_refs[c][rs, ls] = (_gelu(act_refs[c][rs, ls]) * w).astype(BF16)

    def mix(c, pair, assign):
        rs = slice(pair * 2 * nk, (pair + 1) * 2 * nk)
        upd = _dot(vt_ref[:, rs], p_refs[c][rs, :])
        if assign:
            o_ref[:, cols(c)] = upd
        else:
            o_ref[:, cols(c)] += upd

    def step(first_group):
        scores(0)
        for c in range(nchunk):
            if c + 1 < nchunk:
                scores(c + 1)
            for pair in range(PEER_GROUP // 2):
                weigh(c, 2 * pair)
                weigh(c, 2 * pair + 1)
                if pair >= 1:
                    mix(c, pair - 1, first_group and pair == 1)
            mix(c, PEER_GROUP // 2 - 1, False)

    @pl.when(g == 0)
    def _():
        step(True)

    @pl.when(g != 0)
    def _():
        step(False)


def _peer_mix(hb, u_b, vt_b, t1t, d1t, e2t):
    n, d = hb.shape
    tb = min(1024, n)
    ne = u_b.shape[0]
    eb = PEER_GROUP * PEER_N_KEYS
    ng = ne // eb
    gspec = pl.BlockSpec((PEER_HEADS, PEER_GROUP, tb), lambda t, g: (0, g, t))
    kspec = pl.BlockSpec((PEER_HEADS, PEER_N_KEYS, tb), lambda t, g: (0, 0, t))
    nchunk = tb // PEER_LANES
    return pl.pallas_call(
        _peer_mix_kernel,
        grid=(n // tb, ng),
        in_specs=[pl.BlockSpec((tb, d), lambda t, g: (t, 0)),
                  pl.BlockSpec((eb, d), lambda t, g: (g, 0)),
                  pl.BlockSpec((d, eb), lambda t, g: (0, g)),
                  gspec, gspec, kspec],
        out_specs=pl.BlockSpec((d, tb), lambda t, g: (0, t)),
        out_shape=jax.ShapeDtypeStruct((d, n), F32),
        scratch_shapes=([pltpu.VMEM((eb, PEER_LANES), F32)] * nchunk
                        + [pltpu.VMEM((eb, PEER_LANES), BF16)] * nchunk),
        compiler_params=_cparams(("parallel", "arbitrary"), 60),
        name="peer_mix",
    )(hb, u_b, vt_b, t1t, d1t, e2t)


def _final_ln_kernel(h_ref, ft_ref, w_ref, b_ref, o_ref):
    y = DEEPNORM_ALPHA * h_ref[...] + ft_ref[...].T
    o_ref[...] = _layer_norm_rows(y, w_ref[...], b_ref[...])


def _final_ln(h, ffn_t, w, b):
    n, d = h.shape
    tm = min(512, n)
    return pl.pallas_call(
        _final_ln_kernel,
        grid=(n // tm,),
        in_specs=[pl.BlockSpec((tm, d), lambda i: (i, 0)),
                  pl.BlockSpec((d, tm), lambda i: (0, i)),
                  pl.BlockSpec((1, d), lambda i: (0, 0)),
                  pl.BlockSpec((1, d), lambda i: (0, 0))],
        out_specs=pl.BlockSpec((tm, d), lambda i: (i, 0)),
        out_shape=jax.ShapeDtypeStruct((n, d), F32),
        compiler_params=_cparams(("parallel",), 40),
        name="final_ln",
    )(h, ffn_t, w.reshape(1, d), b.reshape(1, d))


def _reorder_w_in(w_in):
    o_ab = 4 * GDN_WIDTH
    d = w_in.shape[0]
    main = jnp.concatenate([w_in[:, :o_ab], w_in[:, o_ab + 2 * GDN_HEADS:]], axis=1)
    ab = jnp.zeros((d, PROJ_COLS - COL_AB), w_in.dtype).at[:, :2 * GDN_HEADS].set(w_in[:, o_ab:o_ab + 2 * GDN_HEADS])
    return jnp.concatenate([main, ab], axis=1)


def kernel(x, mem, ln_emb_w, ln_emb_b, w_in, conv_w, a_log, dt_bias, gdn_norm_w, sb_norm_w,
           mem_norm_w, ln_mem_w, ln_mem_b, w_mem_kv, w_out, ln1_w, ln1_b, peer_wq, peer_keys1,
           peer_keys2, peer_u, peer_v, ln2_w, ln2_b):
    batch, seq, d = x.shape
    n_mem = mem.shape[1]
    l = 0
    x2d = x.reshape(batch * seq, d)
    proj = _ln_mm(x2d, ln_emb_w, ln_emb_b, _reorder_w_in(w_in[l].astype(BF16)), PROJ_TILE)

    o_gdn = _gdn(proj, conv_w[l], a_log[l], dt_bias[l], gdn_norm_w[l], batch, seq)
    o_sb = _sb(proj, sb_norm_w[l], batch, seq)
    _, mb = _ln(mem.reshape(batch * n_mem, d), ln_mem_w[l], ln_mem_b[l])
    kv = _mm(mb, w_mem_kv[l].astype(BF16), 2 * MEM_WIDTH // 2, "mem_kv")
    o_mem = _mem_attn(proj, kv, mem_norm_w[l], batch, seq, n_mem)

    h1, h1b = _out_ln(o_gdn, o_sb, o_mem, w_out[l].astype(BF16), x2d, ln_emb_w, ln_emb_b, ln1_w[l], ln1_b[l])

    s1t, s2t = _peer_scores(h1b, peer_wq[l].astype(BF16), peer_keys1[l].astype(BF16), peer_keys2[l].astype(BF16))
    t1t, d1t, e2t = _peer_gates(s1t, s2t)
    ffn_t = _peer_mix(h1b, peer_u[l].astype(BF16), peer_v[l].astype(BF16).T, t1t, d1t, e2t)
    out = _final_ln(h1, ffn_t, ln2_w[l], ln2_b[l])
    return out.reshape(batch, seq, d)
```
